```python
import math, functools
import jax, jax.numpy as jnp
from jax import lax
import numpy as np

D_MODEL = 4096
BATCH = 1
SEQ = 8192
DEPTH = 1
DEC_BATCH = 128
DEC_SEQ = 4
PAST_LEN = 2048
PAGE_SIZE = 128

N_HEADS = 16
HEAD_DIM = 128
ATTN_WIDTH = N_HEADS * HEAD_DIM
CONV_WIDTH = D_MODEL // 2
CONV_K = 3
MOBA_BLOCK = 256
MOBA_TOPK = 3
N_BUCKETS = 32
MAX_DISTANCE = 128
Q_CHUNK = 32
EPS = 1e-6
SPLIT_SIZES = [ATTN_WIDTH] * 4 + [CONV_WIDTH] * 4 + [D_MODEL] * 2
IN_WIDTH = sum(SPLIT_SIZES)

kernel_name = 'moba_shortconv_gated_hybrid_step'


def rmsnorm(x, g):
    xf = x.astype(jnp.float32)
    y = xf * lax.rsqrt(jnp.mean(xf * xf, axis=-1, keepdims=True) + EPS)
    return (y * g.astype(jnp.float32)).astype(x.dtype)


def t5_bucket(dist):
    max_exact = N_BUCKETS // 2
    n = jnp.maximum(dist, 0)
    nf = jnp.maximum(n, 1).astype(jnp.float32)
    large = max_exact + (jnp.log(nf / max_exact) / math.log(MAX_DISTANCE / max_exact)
                         * (N_BUCKETS - max_exact)).astype(jnp.int32)
    large = jnp.minimum(large, N_BUCKETS - 1)
    return jnp.where(n < max_exact, n, large)


def to_blocks(k_pad):
    nb = k_pad.shape[0] // MOBA_BLOCK
    return k_pad.reshape(nb, MOBA_BLOCK, N_HEADS, HEAD_DIM).transpose(2, 0, 1, 3)


def moba_attend(q, q_pos, kb, vb, rel_bias):
    tq = q.shape[0]
    nb = kb.shape[1]
    n_sel = min(MOBA_TOPK, nb)
    kmean = jnp.mean(kb.astype(jnp.float32), axis=2)
    own = q_pos // MOBA_BLOCK
    gate = jnp.einsum('qhd,hnd->qhn', q.astype(jnp.float32), kmean)
    fully_past = jnp.arange(nb)[None, None, :] < own[:, None, None]
    gate = jnp.where(fully_past, gate, -jnp.inf)
    _, sel = lax.top_k(gate, n_sel)
    sel_ok = sel < own[:, None, None]
    own_b = jnp.broadcast_to(own[:, None, None], (tq, N_HEADS, 1))
    blocks = jnp.concatenate([sel, own_b], axis=-1)
    ok = jnp.concatenate([sel_ok, jnp.ones_like(own_b, dtype=bool)], axis=-1)
    hidx = jnp.arange(N_HEADS)[None, :, None]
    kg = kb[hidx, blocks]
    vg = vb[hidx, blocks]
    key_pos = blocks[..., None] * MOBA_BLOCK + jnp.arange(MOBA_BLOCK)
    dist = q_pos[:, None, None, None] - key_pos
    mask = ok[..., None] & (dist >= 0)
    bias = rel_bias.T[hidx[..., None], t5_bucket(dist)]
    logits = (jnp.einsum('qhd,qhsbd->qhsb', q, kg).astype(jnp.float32) * (HEAD_DIM ** -0.5)
              + bias.astype(jnp.float32))
    logits = jnp.where(mask, logits, -jnp.inf)
    p = jax.nn.softmax(logits.reshape(tq, N_HEADS, -1), axis=-1).reshape(logits.shape)
    return jnp.einsum('qhsb,qhsbd->qhd', p.astype(vg.dtype), vg)


def attend_prompt(q, k, v, rel_bias):
    b, t = q.shape[:2]
    lp = -(-t // MOBA_BLOCK) * MOBA_BLOCK
    pad = ((0, 0), (0, lp - t), (0, 0), (0, 0))
    kb = jax.vmap(to_blocks)(jnp.pad(k, pad))
    vb = jax.vmap(to_blocks)(jnp.pad(v, pad))
    n_chunks = t // Q_CHUNK
    qc = q.reshape(b, n_chunks, Q_CHUNK, N_HEADS, HEAD_DIM).transpose(1, 0, 2, 3, 4)
    pos = jnp.arange(t, dtype=jnp.int32).reshape(n_chunks, Q_CHUNK)

    def step(args):
        qi, pi = args
        return jax.vmap(moba_attend, in_axes=(0, None, 0, 0, None))(qi, pi, kb, vb, rel_bias)

    out = lax.map(step, (qc, pos))
    return out.transpose(1, 0, 2, 3, 4).reshape(b, t, N_HEADS, HEAD_DIM)


def attend_sample(q, k, v, cache_k, cache_v, layer, page_table, rel_bias):
    tn = q.shape[1]
    past = page_table.shape[1] * PAGE_SIZE
    total = past + tn
    lp = -(-total // MOBA_BLOCK) * MOBA_BLOCK
    q_pos = past + jnp.arange(tn, dtype=jnp.int32)

    def step(args):
        qi, kn, vn, pt = args
        kp = cache_k[layer, pt].reshape(past, N_HEADS, HEAD_DIM)
        vp = cache_v[layer, pt].reshape(past, N_HEADS, HEAD_DIM)
        pad = ((0, lp - total), (0, 0), (0, 0))
        k_all = jnp.pad(jnp.concatenate([kp.astype(kn.dtype), kn], axis=0), pad)
        v_all = jnp.pad(jnp.concatenate([vp.astype(vn.dtype), vn], axis=0), pad)
        return moba_attend(qi, q_pos, to_blocks(k_all), to_blocks(v_all), rel_bias)

    return lax.map(step, (q, k, v, page_table))


def short_conv(u, buf, w):
    t = u.shape[1]
    up = jnp.concatenate([buf.astype(u.dtype), u], axis=1)
    y = w[0] * up[:, 0:t]
    for j in range(1, CONV_K):
        y = y + w[j] * up[:, j:j + t]
    return y, up[:, -(CONV_K - 1):]


def mixer_layer(x, gain, w_in, conv_w, w_attn_out, w_conv_out, w_out, attend, conv_buf):
    b, t = x.shape[:2]
    h = rmsnorm(x, gain)
    p = h @ w_in
    offs = [int(o) for o in np.cumsum(SPLIT_SIZES)[:-1]]
    q, k, v, g_a, b_c, c_c, h_c, g_c, m_a, m_c = jnp.split(p, offs, axis=-1)
    q = q.reshape(b, t, N_HEADS, HEAD_DIM)
    k = k.reshape(b, t, N_HEADS, HEAD_DIM)
    v = v.reshape(b, t, N_HEADS, HEAD_DIM)
    a = attend(q, k, v).reshape(b, t, ATTN_WIDTH) * jax.nn.silu(g_a)
    y_c, new_buf = short_conv(c_c * h_c, conv_buf, conv_w)
    c = b_c * y_c * jax.nn.silu(g_c)
    merged = jax.nn.sigmoid(m_a) * (a @ w_attn_out) + jax.nn.sigmoid(m_c) * (c @ w_conv_out)
    return x + merged @ w_out, k, v, new_buf


def setup_inputs(seed: int = 0) -> dict:
    key = jax.random.key(seed)
    ks = jax.random.split(key, 16)
    n_pages = PAST_LEN // PAGE_SIZE
    used = DEC_BATCH * n_pages
    n_pool = used + max(1, used // 4)
    page_table = jax.random.permutation(ks[0], n_pool)[:used].reshape(DEC_BATCH, n_pages).astype(jnp.int32)
    f32 = jnp.float32
    return {
        'x_prompt': jax.random.normal(ks[1], (BATCH, SEQ, D_MODEL), f32),
        'x_sample': jax.random.normal(ks[2], (DEC_BATCH, DEC_SEQ, D_MODEL), f32),
        'cache_k': jax.random.normal(ks[3], (DEPTH, n_pool, PAGE_SIZE, N_HEADS, HEAD_DIM), f32),
        'cache_v': jax.random.normal(ks[4], (DEPTH, n_pool, PAGE_SIZE, N_HEADS, HEAD_DIM), f32),
        'state_conv': jax.random.normal(ks[5], (DEPTH, DEC_BATCH, CONV_K - 1, CONV_WIDTH), f32),
        'page_table': page_table,
        'norm_gain': 1.0 + 0.05 * jax.random.normal(ks[6], (DEPTH, D_MODEL), f32),
        'w_in': jax.random.normal(ks[7], (DEPTH, D_MODEL, IN_WIDTH), f32) * D_MODEL ** -0.5,
        'conv_w': jax.random.normal(ks[8], (DEPTH, CONV_K, CONV_WIDTH), f32) * CONV_K ** -0.5,
        'w_attn_out': jax.random.normal(ks[9], (DEPTH, ATTN_WIDTH, D_MODEL), f32) * ATTN_WIDTH ** -0.5,
        'w_conv_out': jax.random.normal(ks[10], (DEPTH, CONV_WIDTH, D_MODEL), f32) * CONV_WIDTH ** -0.5,
        'w_out': jax.random.normal(ks[11], (DEPTH, D_MODEL, D_MODEL), f32) * D_MODEL ** -0.5,
        'rel_bias': 0.5 * jax.random.normal(ks[12], (N_BUCKETS, N_HEADS), f32),
        'final_gain': 1.0 + 0.05 * jax.random.normal(ks[13], (D_MODEL,), f32),
    }


def reference(x_prompt, x_sample, cache_k, cache_v, state_conv, page_table, norm_gain, w_in,
              conv_w, w_attn_out, w_conv_out, w_out, rel_bias, final_gain):
    yp, ys = x_prompt, x_sample
    kp_l, vp_l, cp_l, ks_l, vs_l, cs_l = [], [], [], [], [], []
    for l in range(DEPTH):
        zero_buf = jnp.zeros((yp.shape[0], CONV_K - 1, CONV_WIDTH), yp.dtype)
        yp, kp, vp, cp = mixer_layer(
            yp, norm_gain[l], w_in[l], conv_w[l], w_attn_out[l], w_conv_out[l], w_out[l],
            functools.partial(attend_prompt, rel_bias=rel_bias), zero_buf)
        ys, kn, vn, cn = mixer_layer(
            ys, norm_gain[l], w_in[l], conv_w[l], w_attn_out[l], w_conv_out[l], w_out[l],
            functools.partial(attend_sample, cache_k=cache_k, cache_v=cache_v, layer=l,
                              page_table=page_table, rel_bias=rel_bias), state_conv[l])
        kp_l.append(kp); vp_l.append(vp); cp_l.append(cp)
        ks_l.append(kn); vs_l.append(vn); cs_l.append(cn)
    y_prompt = rmsnorm(yp, final_gain)
    y_sample = rmsnorm(ys, final_gain)
    return (y_prompt, y_sample, jnp.stack(kp_l), jnp.stack(vp_l), jnp.stack(cp_l),
            jnp.stack(ks_l), jnp.stack(vs_l), jnp.stack(cs_l))
```

```python
import functools
import math

import numpy as np
import jax
import jax.numpy as jnp
from jax import lax
from jax.experimental import pallas as pl
from jax.experimental.pallas import tpu as pltpu

MOBA_BLOCK = 256
MOBA_TOPK = 3
N_BUCKETS = 32
MAX_DISTANCE = 128
EPS = 1e-6
MASKED = -1e30
LANES = 128
BF16_SUBLANES = 16
NORM_ROWS = 64
VMEM_LIMIT_BYTES = 56 * 1024 * 1024

_BF16 = jnp.bfloat16
_F32 = jnp.float32


def _params(*semantics):
    return pltpu.CompilerParams(dimension_semantics=semantics, vmem_limit_bytes=VMEM_LIMIT_BYTES)


def _tile(dim, want):
    t = min(dim, want)
    while dim % t:
        t -= LANES
        assert t > 0, (dim, want)
    return t


def _rmsnorm_kernel(x_ref, g_ref, o_ref):
    x = x_ref[...]
    ms = jnp.mean(x * x, axis=-1, keepdims=True)
    o_ref[...] = ((x * lax.rsqrt(ms + EPS)) * g_ref[...]).astype(o_ref.dtype)


def _rmsnorm_bf16(x, gain):
    m, d = x.shape
    tm = _tile(m, 256)
    return pl.pallas_call(
        _rmsnorm_kernel,
        grid=(m // tm,),
        in_specs=[pl.BlockSpec((tm, d), lambda i: (i, 0)), pl.BlockSpec((1, d), lambda i: (0, 0))],
        out_specs=pl.BlockSpec((tm, d), lambda i: (i, 0)),
        out_shape=jax.ShapeDtypeStruct((m, d), _BF16),
        compiler_params=_params("parallel"),
        name="rmsnorm",
    )(x, gain.reshape(1, d))


def _proj_kernel(h_ref, w_ref, o_ref):
    o_ref[...] = jnp.dot(h_ref[...], w_ref[...], preferred_element_type=_F32).astype(o_ref.dtype)


def _proj(h, w, col0, ncols, name):
    m, k = h.shape
    tm = _tile(m, 1024)
    tn = _tile(math.gcd(ncols, col0), 1024)
    c0 = col0 // tn
    return pl.pallas_call(
        _proj_kernel,
        grid=(ncols // tn, m // tm),
        in_specs=[pl.BlockSpec((tm, k), lambda j, i: (i, 0)),
                  pl.BlockSpec((k, tn), lambda j, i: (0, c0 + j))],
        out_specs=pl.BlockSpec((tm, tn), lambda j, i: (i, j)),
        out_shape=jax.ShapeDtypeStruct((m, ncols), _F32),
        compiler_params=_params("parallel", "arbitrary"),
        name=name,
    )(h, w)


def _t5_bucket(dist):
    max_exact = N_BUCKETS // 2
    n = jnp.maximum(dist, 0)
    nf = jnp.maximum(n, 1).astype(_F32)
    large = max_exact + (jnp.log(nf / max_exact) / math.log(MAX_DISTANCE / max_exact)
                         * (N_BUCKETS - max_exact)).astype(jnp.int32)
    large = jnp.minimum(large, N_BUCKETS - 1)
    return jnp.where(n < max_exact, n, large)


def _bias_of_dist(rel_bias, dist):
    dist = jnp.asarray(dist, jnp.int32)
    b = jnp.moveaxis(rel_bias[_t5_bucket(dist)], -1, 0)
    return jnp.where(dist >= 0, b, MASKED).astype(_F32)


def _topk_mask(gate, n_valid):
    col = lax.broadcasted_iota(jnp.int32, gate.shape, 1).astype(_F32)
    g = jnp.where(col < n_valid, gate, -jnp.inf)
    sel = jnp.zeros(gate.shape, _F32)
    for _ in range(MOBA_TOPK):
        mx = jnp.max(g, axis=-1, keepdims=True)
        first = jnp.min(jnp.where(g == mx, col, float(gate.shape[1])), axis=-1, keepdims=True)
        pick = col == jnp.where(mx > -jnp.inf, first, -1.0)
        sel = jnp.where(pick, 1.0, sel)
        g = jnp.where(pick, -jnp.inf, g)
    return sel


def _lane_column(x, j):
    col = lax.broadcasted_iota(jnp.int32, x.shape, 1)
    return jnp.sum(jnp.where(col == j, x, 0.0), axis=-1, keepdims=True)


def _nt_dot(a, b):
    return lax.dot_general(a, b, (((1,), (1,)), ((), ())), preferred_element_type=_F32)


def _attn_prompt_kernel(far_ref, q_ref, k_ref, v_ref, g_ref, bown_ref, bprev_ref, o_ref,
                        kb_ref, vb_ref, km_ref, *, scale):
    h = pl.program_id(0)
    i = pl.program_id(1)
    blk = MOBA_BLOCK
    nb = kb_ref.shape[0] // blk
    hd = kb_ref.shape[1]

    @pl.when(i == 0)
    def _():
        kf = k_ref[...]
        kb_ref[...] = kf.astype(_BF16)
        vb_ref[...] = v_ref[...].astype(_BF16)
        km_ref[...] = jnp.mean(kf.reshape(nb, blk, hd), axis=1)

    q = q_ref[...]
    qb = q.astype(_BF16)
    gate = lax.dot_general(q, km_ref[...], (((1,), (1,)), ((), ())),
                           precision=lax.Precision.HIGHEST, preferred_element_type=_F32)
    sel = _topk_mask(gate, i.astype(_F32))
    far_bias = far_ref[h]

    def block_rows(j):
        return pl.ds(pl.multiple_of(j * blk, blk), blk)

    def update(carry, s, vj):
        m, l, acc = carry
        m_new = jnp.maximum(m, jnp.max(s, axis=-1, keepdims=True))
        alpha = jnp.exp(m - m_new)
        p = jnp.exp(s - m_new)
        l = alpha * l + jnp.sum(p, axis=-1, keepdims=True)
        acc = alpha * acc + jnp.dot(p.astype(_BF16), vj, preferred_element_type=_F32)
        return m_new, l, acc

    s = _nt_dot(qb, kb_ref[block_rows(i), :]) * scale + bown_ref[...]
    m0 = jnp.max(s, axis=-1, keepdims=True)
    p0 = jnp.exp(s - m0)
    carry = (m0, jnp.sum(p0, axis=-1, keepdims=True),
             jnp.dot(p0.astype(_BF16), vb_ref[block_rows(i), :], preferred_element_type=_F32))

    def prev_block(carry):
        j = i - 1
        pen = jnp.where(_lane_column(sel, j) > 0.0, 0.0, MASKED)
        s = _nt_dot(qb, kb_ref[block_rows(j), :]) * scale + bprev_ref[...] + pen
        return update(carry, s, vb_ref[block_rows(j), :])

    carry = lax.cond(i >= 1, prev_block, lambda c: c, carry)

    def far_block(j, carry):
        pen = jnp.where(_lane_column(sel, j) > 0.0, far_bias, MASKED)
        s = _nt_dot(qb, kb_ref[block_rows(j), :]) * scale + pen
        return update(carry, s, vb_ref[block_rows(j), :])

    m, l, acc = lax.fori_loop(0, jnp.maximum(i - 1, 0), far_block, carry)
    g = g_ref[...]
    o_ref[...] = ((acc / l) * (g * jax.nn.sigmoid(g))).astype(o_ref.dtype)


def _attn_prompt(q, k, v, rest, rel_bias, n_heads, hd):
    t = q.shape[0]
    blk = MOBA_BLOCK
    assert t % blk == 0 and hd == LANES
    nb = t // blk
    r = np.arange(blk)
    d_own = r[:, None] - r[None, :]
    bown = _bias_of_dist(rel_bias, d_own)
    bprev = _bias_of_dist(rel_bias, d_own + blk)
    assert blk + 1 >= MAX_DISTANCE
    far = rel_bias[N_BUCKETS - 1].astype(_F32)
    kernel = functools.partial(_attn_prompt_kernel, scale=hd ** -0.5)
    return pl.pallas_call(
        kernel,
        grid=(n_heads, nb),
        in_specs=[pl.BlockSpec(memory_space=pltpu.SMEM),
                  pl.BlockSpec((blk, hd), lambda h, i: (i, h)),
                  pl.BlockSpec((t, hd), lambda h, i: (0, h)),
                  pl.BlockSpec((t, hd), lambda h, i: (0, h)),
                  pl.BlockSpec((blk, hd), lambda h, i: (i, h)),
                  pl.BlockSpec((None, blk, blk), lambda h, i: (h, 0, 0)),
                  pl.BlockSpec((None, blk, blk), lambda h, i: (h, 0, 0))],
        out_specs=pl.BlockSpec((blk, hd), lambda h, i: (i, h)),
        out_shape=jax.ShapeDtypeStruct((t, n_heads * hd), _BF16),
        scratch_shapes=[pltpu.VMEM((t, hd), _BF16), pltpu.VMEM((t, hd), _BF16),
                        pltpu.VMEM((nb, hd), _F32)],
        compiler_params=_params("parallel", "arbitrary"),
        name="attn_prompt",
    )(far, q, k, v, rest, bown, bprev)


def _attn_sample_kernel(pt_ref, qbd_ref, k0_ref, k1_ref, v0_ref, v1_ref, bias_ref, kn_ref, vn_ref,
                        bnew_ref, g_ref, o_ref, gate_ref, mblk_ref, lblk_ref, pv_ref,
                        *, scale, n_heads, hd, tn):
    del pt_ref
    j = pl.program_id(1)
    nblk = pv_ref.shape[0]
    rows = n_heads * tn
    qbd = qbd_ref[...]

    def head_diagonal(x):
        row_head = lax.broadcasted_iota(jnp.int32, (rows, hd), 0) // tn
        out = jnp.zeros((rows, hd), _F32)
        for hh in range(n_heads):
            out = out + jnp.where(row_head == hh, x[:, hh * hd:(hh + 1) * hd], 0.0)
        return out

    @pl.when(j == 0)
    def _():
        gate_ref[...] = jnp.zeros_like(gate_ref)
        mblk_ref[...] = jnp.zeros_like(mblk_ref)
        lblk_ref[...] = jnp.zeros_like(lblk_ref)

    kb = jnp.concatenate([k0_ref[...], k1_ref[...]], axis=0).astype(_BF16)
    vb = jnp.concatenate([v0_ref[...], v1_ref[...]], axis=0).astype(_BF16)
    s_raw = _nt_dot(qbd, kb)
    gsum = jnp.sum(s_raw, axis=-1, keepdims=True)
    s = s_raw * scale + bias_ref[...]
    mj = jnp.max(s, axis=-1, keepdims=True)
    p = jnp.exp(s - mj)
    lj = jnp.sum(p, axis=-1, keepdims=True)
    pv_ref[j] = head_diagonal(jnp.dot(p.astype(_BF16), vb, preferred_element_type=_F32))
    lane = lax.broadcasted_iota(jnp.int32, gate_ref.shape, 1)
    gate_ref[...] = jnp.where(lane == j, gsum, gate_ref[...])
    mblk_ref[...] = jnp.where(lane == j, mj, mblk_ref[...])
    lblk_ref[...] = jnp.where(lane == j, lj, lblk_ref[...])

    @pl.when(j == nblk - 1)
    def _():
        sel = _topk_mask(gate_ref[...], float(nblk)) > 0.0
        s_new = _nt_dot(qbd, kn_ref[...]) * scale + bnew_ref[...]
        m_new = jnp.max(s_new, axis=-1, keepdims=True)
        mblk = mblk_ref[...]
        m_tot = jnp.maximum(m_new, jnp.max(jnp.where(sel, mblk, MASKED), axis=-1, keepdims=True))
        w = jnp.where(sel, jnp.exp(mblk - m_tot), 0.0)
        p_new = jnp.exp(s_new - m_tot)
        l = jnp.sum(w * lblk_ref[...], axis=-1, keepdims=True) + jnp.sum(p_new, axis=-1, keepdims=True)
        acc = head_diagonal(jnp.dot(p_new.astype(_BF16), vn_ref[...], preferred_element_type=_F32))
        for jj in range(nblk):
            acc = acc + w[:, jj:jj + 1] * pv_ref[jj]
        out = acc / l
        g = g_ref[...]
        gated = g * jax.nn.sigmoid(g)
        for hh in range(n_heads):
            cols = slice(hh * hd, (hh + 1) * hd)
            o_ref[:, cols] = (out[hh * tn:(hh + 1) * tn, :] * gated[:, cols]).astype(o_ref.dtype)


def _attn_sample(q, k_new, v_new, rest, cache_k, cache_v, page_table, rel_bias, n_heads, hd):
    b, tn, width = q.shape
    n_pages = page_table.shape[1]
    page = cache_k.shape[1]
    blk = MOBA_BLOCK
    past = n_pages * page
    assert blk % page == 0 and blk // page == 2 and past % blk == 0 and tn <= BF16_SUBLANES
    assert hd == LANES
    nblk = past // blk
    assert nblk <= LANES
    rows = n_heads * tn

    head_of_row = np.repeat(np.arange(n_heads), tn)
    head_of_col = np.repeat(np.arange(n_heads), hd)
    diag = jnp.asarray(head_of_row[:, None] == head_of_col[None, :])
    qbd = jnp.where(diag[None], jnp.tile(q.astype(_BF16), (1, n_heads, 1)), 0).astype(_BF16)
    pad = ((0, 0), (0, BF16_SUBLANES - tn), (0, 0))
    kn = jnp.pad(k_new.astype(_BF16), pad)
    vn = jnp.pad(v_new.astype(_BF16), pad)

    tq = np.arange(tn)
    d_past = (past + tq)[None, :, None] - (np.arange(nblk)[:, None, None] * blk + np.arange(blk)[None, None, :])
    bias_past = _bias_of_dist(rel_bias, d_past)
    bias_past = bias_past.transpose(1, 0, 2, 3).reshape(nblk, rows, blk)
    tk = np.arange(BF16_SUBLANES)
    d_new = np.where(tk[None, :] < tn, tq[:, None] - tk[None, :], -1)
    bias_new = _bias_of_dist(rel_bias, d_new).reshape(rows, BF16_SUBLANES)

    kernel = functools.partial(_attn_sample_kernel, scale=hd ** -0.5, n_heads=n_heads, hd=hd, tn=tn)
    page_spec = lambda s: pl.BlockSpec((None, page, width), lambda bi, j, pt: (pt[bi, 2 * j + s], 0, 0))
    per_b = lambda shape: pl.BlockSpec((None,) + shape, lambda bi, j, pt: (bi, 0, 0))
    grid_spec = pltpu.PrefetchScalarGridSpec(
        num_scalar_prefetch=1,
        grid=(b, nblk),
        in_specs=[per_b((rows, width)),
                  page_spec(0), page_spec(1), page_spec(0), page_spec(1),
                  pl.BlockSpec((None, rows, blk), lambda bi, j, pt: (j, 0, 0)),
                  per_b((BF16_SUBLANES, width)), per_b((BF16_SUBLANES, width)),
                  pl.BlockSpec((rows, BF16_SUBLANES), lambda bi, j, pt: (0, 0)),
                  per_b((tn, width))],
        out_specs=per_b((tn, width)),
        scratch_shapes=[pltpu.VMEM((rows, LANES), _F32), pltpu.VMEM((rows, LANES), _F32),
                        pltpu.VMEM((rows, LANES), _F32), pltpu.VMEM((nblk, rows, hd), _F32)],
    )
    return pl.pallas_call(
        kernel,
        grid_spec=grid_spec,
        out_shape=jax.ShapeDtypeStruct((b, tn, width), _F32),
        compiler_params=_params("parallel", "arbitrary"),
        name="attn_sample",
    )(page_table, qbd, cache_k, cache_k, cache_v, cache_v, bias_past, kn, vn, bias_new, rest).astype(_BF16)


def _conv_combine(u, prev1, prev2, w_ref, b_ref, g_ref, o_ref):
    y = w_ref[0:1, :] * prev2 + w_ref[1:2, :] * prev1 + w_ref[2:3, :] * u
    g = g_ref[...]
    o_ref[...] = (b_ref[...] * y * (g * jax.nn.sigmoid(g))).astype(o_ref.dtype)


def _conv_prompt_kernel(b_ref, c_ref, h_ref, g_ref, ch_ref, hh_ref, w_ref, o_ref, st_ref):
    i = pl.program_id(1)
    u = c_ref[...] * h_ref[...]
    halo = ch_ref[...] * hh_ref[...] * (i > 0).astype(_F32)
    row = lax.broadcasted_iota(jnp.int32, u.shape, 0)
    prev1 = jnp.where(row == 0, halo[7:8, :], pltpu.roll(u, 1, axis=0))
    prev2 = jnp.where(row == 0, halo[6:7, :], jnp.where(row == 1, halo[7:8, :], pltpu.roll(u, 2, axis=0)))
    _conv_combine(u, prev1, prev2, w_ref, b_ref, g_ref, o_ref)
    st_ref[...] = u[u.shape[0] - 8:, :]


def _conv_sample_kernel(b_ref, c_ref, h_ref, g_ref, s1_ref, s2_ref, w_ref, o_ref, u_ref, *, tn):
    u = c_ref[...] * h_ref[...]
    t = lax.broadcasted_iota(jnp.int32, u.shape, 0) % tn
    prev1 = jnp.where(t >= 1, pltpu.roll(u, 1, axis=0), s1_ref[...])
    prev2 = jnp.where(t >= 2, pltpu.roll(u, 2, axis=0), s2_ref[...])
    _conv_combine(u, prev1, prev2, w_ref, b_ref, g_ref, o_ref)
    u_ref[...] = u


def _rest_cols(cw, tc):
    return [(1 + n) * cw // tc for n in range(4)]


def _conv_prompt(rest, conv_w, cw):
    t = rest.shape[0]
    tm = _tile(t, 512)
    tc = _tile(cw, 512)
    ob, oc, oh, og = _rest_cols(cw, tc)
    main = lambda off: pl.BlockSpec((tm, tc), lambda j, i: (i, off + j))
    halo = lambda off: pl.BlockSpec((8, tc), lambda j, i: (jnp.maximum(i * (tm // 8) - 1, 0), off + j))
    return pl.pallas_call(
        _conv_prompt_kernel,
        grid=(cw // tc, t // tm),
        in_specs=[main(ob), main(oc), main(oh), main(og), halo(oc), halo(oh),
                  pl.BlockSpec((conv_w.shape[0], tc), lambda j, i: (0, j))],
        out_specs=[pl.BlockSpec((tm, tc), lambda j, i: (i, j)),
                   pl.BlockSpec((8, tc), lambda j, i: (0, j))],
        out_shape=[jax.ShapeDtypeStruct((t, cw), _BF16), jax.ShapeDtypeStruct((8, cw), _F32)],
        compiler_params=_params("parallel", "arbitrary"),
        name="conv_prompt",
    )(rest, rest, rest, rest, rest, rest, conv_w)


def _conv_sample(rest, state, conv_w, cw, tn):
    m = rest.shape[0]
    b = m // tn
    assert tn >= 2 and conv_w.shape[0] == 3
    tc = _tile(cw, 512)
    ob, oc, oh, og = _rest_cols(cw, tc)
    zeros = jnp.zeros((b, tn - 1, cw), _F32)
    s1 = jnp.concatenate([state[:, 1:2], zeros], axis=1).reshape(m, cw)
    s2 = jnp.concatenate([state, zeros[:, 1:]], axis=1).reshape(m, cw)
    main = lambda off: pl.BlockSpec((m, tc), lambda j: (0, off + j))
    own = pl.BlockSpec((m, tc), lambda j: (0, j))
    return pl.pallas_call(
        functools.partial(_conv_sample_kernel, tn=tn),
        grid=(cw // tc,),
        in_specs=[main(ob), main(oc), main(oh), main(og), own, own,
                  pl.BlockSpec((conv_w.shape[0], tc), lambda j: (0, j))],
        out_specs=[own, own],
        out_shape=[jax.ShapeDtypeStruct((m, cw), _BF16), jax.ShapeDtypeStruct((m, cw), _F32)],
        compiler_params=_params("parallel"),
        name="conv_sample",
    )(rest, rest, rest, rest, s1, s2, conv_w)


def _merge_kernel(a_ref, c_ref, wa_ref, wc_ref, ma_ref, mc_ref, o_ref):
    ya = jnp.dot(a_ref[...], wa_ref[...], preferred_element_type=_F32)
    yc = jnp.dot(c_ref[...], wc_ref[...], preferred_element_type=_F32)
    o_ref[...] = (jax.nn.sigmoid(ma_ref[...]) * ya + jax.nn.sigmoid(mc_ref[...]) * yc).astype(o_ref.dtype)


def _merge(a, c, wa, wc, rest, gate_col0):
    m, ka = a.shape
    kc = c.shape[1]
    d = wa.shape[1]
    tm = _tile(m, 512)
    tn = _tile(math.gcd(d, gate_col0), 1024)
    oa = gate_col0 // tn
    oc = (gate_col0 + d) // tn
    return pl.pallas_call(
        _merge_kernel,
        grid=(d // tn, m // tm),
        in_specs=[pl.BlockSpec((tm, ka), lambda j, i: (i, 0)),
                  pl.BlockSpec((tm, kc), lambda j, i: (i, 0)),
                  pl.BlockSpec((ka, tn), lambda j, i: (0, j)),
                  pl.BlockSpec((kc, tn), lambda j, i: (0, j)),
                  pl.BlockSpec((tm, tn), lambda j, i: (i, oa + j)),
                  pl.BlockSpec((tm, tn), lambda j, i: (i, oc + j))],
        out_specs=pl.BlockSpec((tm, tn), lambda j, i: (i, j)),
        out_shape=jax.ShapeDtypeStruct((m, d), _BF16),
        compiler_params=_params("parallel", "arbitrary"),
        name="merge",
    )(a, c, wa, wc, rest, rest)


def _out_kernel(mg_ref, w_ref, x_ref, g_ref, o_ref, *, tn, nj):
    j = pl.program_id(1)
    z = x_ref[...] + jnp.dot(mg_ref[...], w_ref[...], preferred_element_type=_F32)
    for jj in range(nj):
        @pl.when(j == jj)
        def _(jj=jj):
            o_ref[:, jj * tn:(jj + 1) * tn] = z

    @pl.when(j == nj - 1)
    def _():
        nr = math.gcd(o_ref.shape[0], NORM_ROWS)

        def norm_rows(r, _):
            rows = pl.ds(pl.multiple_of(r * nr, nr), nr)
            y = o_ref[rows, :]
            ms = jnp.mean(y * y, axis=-1, keepdims=True)
            o_ref[rows, :] = (y * lax.rsqrt(ms + EPS)) * g_ref[...]
            return 0

        lax.fori_loop(0, o_ref.shape[0] // nr, norm_rows, 0)


def _out(merged, w, x, gain):
    m, d = x.shape
    tm = _tile(m, 512)
    tn = _tile(d, 512)
    nj = d // tn
    return pl.pallas_call(
        functools.partial(_out_kernel, tn=tn, nj=nj),
        grid=(m // tm, nj),
        in_specs=[pl.BlockSpec((tm, d), lambda i, j: (i, 0)),
                  pl.BlockSpec((d, tn), lambda i, j: (0, j)),
                  pl.BlockSpec((tm, tn), lambda i, j: (i, j)),
                  pl.BlockSpec((1, d), lambda i, j: (0, 0))],
        out_specs=pl.BlockSpec((tm, d), lambda i, j: (i, 0)),
        out_shape=jax.ShapeDtypeStruct((m, d), _F32),
        compiler_params=_params("parallel", "arbitrary"),
        name="out_proj",
    )(merged, w, x, gain.reshape(1, d))


def _layer(x, gain, w_in, conv_w, wa, wc, w_out, out_gain, aw, cw, attend, conv):
    h = _rmsnorm_bf16(x, gain)
    q = _proj(h, w_in, 0, aw, "proj_q")
    k = _proj(h, w_in, aw, aw, "proj_k")
    v = _proj(h, w_in, 2 * aw, aw, "proj_v")
    rest = _proj(h, w_in, 3 * aw, w_in.shape[1] - 3 * aw, "proj_rest")
    a = attend(q, k, v, rest)
    c, state = conv(rest)
    merged = _merge(a, c, wa, wc, rest, aw + 4 * cw)
    return _out(merged, w_out, x, out_gain), k, v, state


def kernel(x_prompt, x_sample, cache_k, cache_v, state_conv, page_table, norm_gain, w_in,
           conv_w, w_attn_out, w_conv_out, w_out, rel_bias, final_gain):
    depth = w_in.shape[0]
    assert depth == 1, "final norm is fused into the (single) layer's output projection"
    bp, t, d = x_prompt.shape
    assert bp == 1
    db, tn, _ = x_sample.shape
    n_pool, page, n_heads, hd = cache_k.shape[1:]
    aw = n_heads * hd
    cw = conv_w.shape[2]
    assert aw == cw

    l = 0
    w_in_b = w_in[l].astype(_BF16)
    wa_b = w_attn_out[l].astype(_BF16)
    wc_b = w_conv_out[l].astype(_BF16)
    wo_b = w_out[l].astype(_BF16)
    ck = cache_k[l].reshape(n_pool, page, aw)
    cv = cache_v[l].reshape(n_pool, page, aw)

    def attend_prompt(q, k, v, rest):
        return _attn_prompt(q, k, v, rest, rel_bias, n_heads, hd)

    def conv_prompt(rest):
        c, st = _conv_prompt(rest, conv_w[l], cw)
        return c, st[8 - (conv_w.shape[1] - 1):]

    yp, kp, vp, cp = _layer(x_prompt.reshape(t, d), norm_gain[l], w_in_b, conv_w[l], wa_b, wc_b, wo_b,
                            final_gain, aw, cw, attend_prompt, conv_prompt)

    def attend_sample(q, k, v, rest):
        r3 = lambda z: z.reshape(db, tn, z.shape[-1])
        return _attn_sample(r3(q), r3(k), r3(v), r3(rest), ck, cv, page_table, rel_bias,
                            n_heads, hd).reshape(db * tn, aw)

    def conv_sample(rest):
        c, u = _conv_sample(rest, state_conv[l], conv_w[l], cw, tn)
        return c, u.reshape(db, tn, cw)[:, tn - (conv_w.shape[1] - 1):]

    ys, ks, vs, cs = _layer(x_sample.reshape(db * tn, d), norm_gain[l], w_in_b, conv_w[l], wa_b, wc_b, wo_b,
                            final_gain, aw, cw, attend_sample, conv_sample)

    return (yp.reshape(1, t, d), ys.reshape(db, tn, d),
            kp.reshape(1, 1, t, n_heads, hd), vp.reshape(1, 1, t, n_heads, hd),
            cp.reshape(1, 1, conv_w.shape[1] - 1, cw),
            ks.reshape(1, db, tn, n_heads, hd), vs.reshape(1, db, tn, n_heads, hd),
            cs.reshape(1, db, conv_w.shape[1] - 1, cw))
```

```python
import functools
import math

import numpy as np
import jax
import jax.numpy as jnp
from jax import lax
from jax.experimental import pallas as pl
from jax.experimental.pallas import tpu as pltpu

MOBA_BLOCK = 256
MOBA_TOPK = 3
N_BUCKETS = 32
MAX_DISTANCE = 128
EPS = 1e-6
MASKED = -1e30
LANES = 128
NORM_ROWS = 64
FAR_GROUP = 4
PAGES_PER_STEP = 4
VMEM_LIMIT_BYTES = 56 * 1024 * 1024

_BF16 = jnp.bfloat16
_F32 = jnp.float32


def _params(*semantics):
    return pltpu.CompilerParams(dimension_semantics=semantics, vmem_limit_bytes=VMEM_LIMIT_BYTES)


def _tile(dim, want):
    t = min(dim, want)
    while dim % t:
        t -= LANES
        assert t > 0, (dim, want)
    return t


def _rmsnorm_kernel(x_ref, g_ref, o_ref):
    x = x_ref[...]
    ms = jnp.mean(x * x, axis=-1, keepdims=True)
    o_ref[...] = ((x * lax.rsqrt(ms + EPS)) * g_ref[...]).astype(o_ref.dtype)


def _rmsnorm_bf16(x, gain):
    m, d = x.shape
    tm = _tile(m, 256)
    return pl.pallas_call(
        _rmsnorm_kernel,
        grid=(m // tm,),
        in_specs=[pl.BlockSpec((tm, d), lambda i: (i, 0)), pl.BlockSpec((1, d), lambda i: (0, 0))],
        out_specs=pl.BlockSpec((tm, d), lambda i: (i, 0)),
        out_shape=jax.ShapeDtypeStruct((m, d), _BF16),
        compiler_params=_params("parallel"),
        name="rmsnorm",
    )(x, gain.reshape(1, d))


def _proj_kernel(h_ref, w_ref, o_ref):
    o_ref[...] = jnp.dot(h_ref[...], w_ref[...], preferred_element_type=_F32).astype(o_ref.dtype)


def _proj(h, w, col0, ncols, name):
    m, k = h.shape
    tm = _tile(m, 1024)
    tn = _tile(math.gcd(ncols, col0), 1024)
    c0 = col0 // tn
    return pl.pallas_call(
        _proj_kernel,
        grid=(ncols // tn, m // tm),
        in_specs=[pl.BlockSpec((tm, k), lambda j, i: (i, 0)),
                  pl.BlockSpec((k, tn), lambda j, i: (0, c0 + j))],
        out_specs=pl.BlockSpec((tm, tn), lambda j, i: (i, j)),
        out_shape=jax.ShapeDtypeStruct((m, ncols), _F32),
        compiler_params=_params("parallel", "arbitrary"),
        name=name,
    )(h, w)


def _t5_bucket(dist):
    max_exact = N_BUCKETS // 2
    n = jnp.maximum(dist, 0)
    nf = jnp.maximum(n, 1).astype(_F32)
    large = max_exact + (jnp.log(nf / max_exact) / math.log(MAX_DISTANCE / max_exact)
                         * (N_BUCKETS - max_exact)).astype(jnp.int32)
    large = jnp.minimum(large, N_BUCKETS - 1)
    return jnp.where(n < max_exact, n, large)


def _bias_table(rel_bias, n_dist):
    return rel_bias[_t5_bucket(jnp.arange(n_dist, dtype=jnp.int32))].T.astype(_F32)


def _toeplitz(w):
    h, two_n = w.shape
    n = two_n // 2
    flat = jnp.tile(w, (1, n))[:, :n * (two_n - 1)]
    return flat.reshape(h, n, two_n - 1)[:, :, :n]


def _topk_mask(gate, n_valid, axis):
    idx = lax.broadcasted_iota(jnp.int32, gate.shape, axis).astype(_F32)
    g = jnp.where(idx < n_valid, gate, -jnp.inf)
    sel = jnp.zeros(gate.shape, _F32)
    for _ in range(MOBA_TOPK):
        mx = jnp.max(g, axis=axis, keepdims=True)
        first = jnp.min(jnp.where(g == mx, idx, float(gate.shape[axis])), axis=axis, keepdims=True)
        pick = idx == jnp.where(mx > -jnp.inf, first, -1.0)
        sel = jnp.where(pick, 1.0, sel)
        g = jnp.where(pick, -jnp.inf, g)
    return sel


def _nt_dot(a, b, precision=None):
    return lax.dot_general(a, b, (((1,), (1,)), ((), ())), precision=precision, preferred_element_type=_F32)


def _attn_prompt_kernel(far_ref, q_ref, k_ref, v_ref, g_ref, bown_ref, bprev_ref, o_ref,
                        kb_ref, vt_ref, km_ref, sel_ref, *, scale):
    h = pl.program_id(0)
    i = pl.program_id(1)
    nb, blk, hd = kb_ref.shape

    @pl.when(i == 0)
    def _():
        def load_block(c, _):
            rows = pl.ds(pl.multiple_of(c * blk, blk), blk)
            kf = k_ref[rows, :]
            kb_ref[c] = kf.astype(_BF16)
            vt_ref[c] = v_ref[rows, :].T.astype(_BF16)
            km_ref[pl.ds(c, 1), :] = jnp.mean(kf, axis=0, keepdims=True)
            return 0

        lax.fori_loop(0, nb, load_block, 0)

    qt = q_ref[...].T
    qtb = qt.astype(_BF16)
    gate = jnp.dot(km_ref[...], qt, precision=lax.Precision.HIGHEST, preferred_element_type=_F32)
    sel_ref[...] = _topk_mask(gate, i.astype(_F32), axis=0)
    far_bias = far_ref[h]

    def scores(j):
        return jnp.dot(kb_ref[j], qtb, preferred_element_type=_F32) * scale

    def weighted_values(j, p):
        return jnp.dot(vt_ref[j], p.astype(_BF16), preferred_element_type=_F32)

    jp = jnp.maximum(i - 1, 0)
    pen_prev = jnp.where(sel_ref[pl.ds(jp, 1), :] > 0.0, 0.0, MASKED)
    s_own = scores(i) + bown_ref[...]
    s_prev = scores(jp) + bprev_ref[...] + pen_prev
    m = jnp.maximum(jnp.max(s_own, axis=0, keepdims=True), jnp.max(s_prev, axis=0, keepdims=True))
    p_own = jnp.exp(s_own - m)
    p_prev = jnp.exp(s_prev - m)
    l = jnp.sum(p_own, axis=0, keepdims=True) + jnp.sum(p_prev, axis=0, keepdims=True)
    acc = weighted_values(i, p_own) + weighted_values(jp, p_prev)

    def far_block(gidx, g):
        return jnp.minimum(gidx * FAR_GROUP + g, nb - 1)

    def far_scores(gidx):
        ss = []
        for g in range(FAR_GROUP):
            jc = far_block(gidx, g)
            visible = gidx * FAR_GROUP + g < i - 1
            pen = jnp.where(sel_ref[pl.ds(jc, 1), :] > 0.0, jnp.where(visible, far_bias, MASKED), MASKED)
            ss.append(scores(jc) + pen)
        return tuple(ss)

    def far_group(gidx, carry):
        m, l, acc, ss = carry
        ss_next = far_scores(gidx + 1)
        m_new = m
        for s in ss:
            m_new = jnp.maximum(m_new, jnp.max(s, axis=0, keepdims=True))
        alpha = jnp.exp(m - m_new)
        l = alpha * l
        acc = alpha * acc
        for g, s in enumerate(ss):
            p = jnp.exp(s - m_new)
            l = l + jnp.sum(p, axis=0, keepdims=True)
            acc = acc + weighted_values(far_block(gidx, g), p)
        return m_new, l, acc, ss_next

    n_groups = (jnp.maximum(i - 1, 0) + FAR_GROUP - 1) // FAR_GROUP
    m, l, acc, _ = lax.fori_loop(0, n_groups, far_group, (m, l, acc, far_scores(0)))
    g = g_ref[...]
    o_ref[...] = ((acc / l).T * (g * jax.nn.sigmoid(g))).astype(o_ref.dtype)


def _attn_prompt(q, k, v, rest, rel_bias, n_heads, hd):
    t = q.shape[0]
    blk = MOBA_BLOCK
    assert t % blk == 0 and hd == LANES
    nb = t // blk
    assert blk + 1 >= MAX_DISTANCE
    tab = _bias_table(rel_bias, 2 * blk)
    bown = _toeplitz(jnp.concatenate([tab[:, :blk], jnp.full((n_heads, blk), MASKED, _F32)], axis=1))
    bprev = _toeplitz(jnp.concatenate([tab[:, blk:], tab[:, :blk]], axis=1))
    far = rel_bias[N_BUCKETS - 1].astype(_F32)
    kernel = functools.partial(_attn_prompt_kernel, scale=hd ** -0.5)
    return pl.pallas_call(
        kernel,
        grid=(n_heads, nb),
        in_specs=[pl.BlockSpec(memory_space=pltpu.SMEM),
                  pl.BlockSpec((blk, hd), lambda h, i: (i, h)),
                  pl.BlockSpec((t, hd), lambda h, i: (0, h)),
                  pl.BlockSpec((t, hd), lambda h, i: (0, h)),
                  pl.BlockSpec((blk, hd), lambda h, i: (i, h)),
                  pl.BlockSpec((None, blk, blk), lambda h, i: (h, 0, 0)),
                  pl.BlockSpec((None, blk, blk), lambda h, i: (h, 0, 0))],
        out_specs=pl.BlockSpec((blk, hd), lambda h, i: (i, h)),
        out_shape=jax.ShapeDtypeStruct((t, n_heads * hd), _BF16),
        scratch_shapes=[pltpu.VMEM((nb, blk, hd), _BF16), pltpu.VMEM((nb, hd, blk), _BF16),
                        pltpu.VMEM((nb, hd), _F32), pltpu.VMEM((nb, blk), _F32)],
        compiler_params=_params("parallel", "arbitrary"),
        name="attn_prompt",
    )(far, q, k, v, rest, bown, bprev)


def _attn_sample_kernel(pt_ref, q_ref, qb_ref, *refs, scale, n_heads, tn, pages_per_block):
    pps = PAGES_PER_STEP
    k_refs, v_refs = refs[:pps], refs[pps:2 * pps]
    (bfar_ref, blast_ref, kn_ref, vn_ref, bnew_ref, g_ref, o_ref, m_ref, l_ref, pv_ref, ks_ref) = refs[2 * pps:]
    del pt_ref
    jj = pl.program_id(1)
    n_pages, rows, hd = pv_ref.shape
    nblk = ks_ref.shape[0]
    n_steps = n_pages // pps
    qb = qb_ref[...]

    @pl.when(jj == 0)
    def _():
        m_ref[...] = jnp.zeros_like(m_ref)
        l_ref[...] = jnp.zeros_like(l_ref)

    lane = lax.broadcasted_iota(jnp.int32, m_ref.shape, 1)
    for s in range(pps):
        page = jj * pps + s
        kf = k_refs[s][...]
        bias = (blast_ref if s == pps - 1 else bfar_ref)[...]
        sc = _nt_dot(qb, kf.astype(_BF16)) * scale + bias
        mp = jnp.max(sc, axis=-1, keepdims=True)
        p = jnp.exp(sc - mp)
        lp = jnp.sum(p, axis=-1, keepdims=True)
        pv_ref[page] = jnp.dot(p.astype(_BF16), v_refs[s][...].astype(_BF16), preferred_element_type=_F32)
        m_ref[...] = jnp.where(lane == page, mp, m_ref[...])
        l_ref[...] = jnp.where(lane == page, lp, l_ref[...])
        ksum = jnp.sum(kf.reshape(kf.shape[0] // n_heads, n_heads, hd), axis=0)
        blk_id = page // pages_per_block
        if s % pages_per_block == 0:
            ks_ref[blk_id] = ksum
        else:
            ks_ref[blk_id] = ks_ref[blk_id] + ksum

    @pl.when(jj == n_steps - 1)
    def _():
        q32 = q_ref[...]
        row_head = lax.broadcasted_iota(jnp.int32, (rows, n_heads), 0) // tn
        head = lax.broadcasted_iota(jnp.int32, (rows, n_heads), 1)
        gate = jnp.zeros((rows, LANES), _F32)
        for b in range(nblk):
            g2 = _nt_dot(q32, ks_ref[b], precision=lax.Precision.HIGHEST)
            gb = jnp.sum(jnp.where(head == row_head, g2, 0.0), axis=-1, keepdims=True)
            gate = jnp.where(lane == b, gb, gate)
        sel = _topk_mask(gate, float(nblk), axis=1)

        s_new = _nt_dot(qb, kn_ref[...]) * scale + bnew_ref[...]
        m_tot = jnp.max(s_new, axis=-1, keepdims=True)
        mstat = m_ref[...]
        lstat = l_ref[...]
        live = [sel[:, pg // pages_per_block:pg // pages_per_block + 1] > 0.0 for pg in range(n_pages)]
        for pg in range(n_pages):
            m_tot = jnp.maximum(m_tot, jnp.where(live[pg], mstat[:, pg:pg + 1], MASKED))
        p_new = jnp.exp(s_new - m_tot)
        l = jnp.sum(p_new, axis=-1, keepdims=True)
        acc = jnp.dot(p_new.astype(_BF16), vn_ref[...], preferred_element_type=_F32)
        for pg in range(n_pages):
            w = jnp.where(live[pg], jnp.exp(mstat[:, pg:pg + 1] - m_tot), 0.0)
            l = l + w * lstat[:, pg:pg + 1]
            acc = acc + w * pv_ref[pg]
        out = acc / l
        g = g_ref[...]
        gated = g * jax.nn.sigmoid(g)
        for hh in range(n_heads):
            cols = slice(hh * hd, (hh + 1) * hd)
            o_ref[:, cols] = (out[hh * tn:(hh + 1) * tn, :] * gated[:, cols]).astype(o_ref.dtype)


def _attn_sample(q, k_new, v_new, rest, cache_k, cache_v, page_table, rel_bias, n_heads, hd):
    b, tn, width = q.shape
    n_pages = page_table.shape[1]
    page = cache_k.shape[1] // n_heads
    blk = MOBA_BLOCK
    pps = PAGES_PER_STEP
    past = n_pages * page
    assert blk % page == 0 and past % blk == 0 and n_pages % pps == 0 and n_pages <= LANES
    assert hd == LANES and page + 1 >= MAX_DISTANCE
    ppb = blk // page
    nblk = past // blk
    rows = n_heads * tn
    cols = page * n_heads

    qh = q.reshape(b, tn, n_heads, hd).transpose(0, 2, 1, 3).reshape(b, rows, hd)
    kn = k_new.reshape(b, tn * n_heads, hd).astype(_BF16)
    vn = v_new.reshape(b, tn * n_heads, hd).astype(_BF16)

    head_of_row = np.repeat(np.arange(n_heads), tn)
    t_of_row = np.tile(np.arange(tn), n_heads)
    diag = jnp.asarray(head_of_row[:, None] == np.tile(np.arange(n_heads), page)[None, :])
    tab = _bias_table(rel_bias, page + tn)
    far = rel_bias[N_BUCKETS - 1].astype(_F32)
    bias_far = jnp.where(diag, far[head_of_row][:, None], MASKED)
    d_last = page + np.arange(tn)[:, None] - np.arange(page)[None, :]
    b_last = tab[:, d_last]
    b_last = jnp.broadcast_to(b_last[:, :, :, None], (n_heads, tn, page, n_heads)).reshape(rows, cols)
    bias_pages = jnp.stack([bias_far, jnp.where(diag, b_last, MASKED)])
    d_new = t_of_row[:, None] - np.repeat(np.arange(tn), n_heads)[None, :]
    same_head = head_of_row[:, None] == np.tile(np.arange(n_heads), tn)[None, :]
    b_new = tab[head_of_row[:, None], np.maximum(d_new, 0)]
    bias_new = jnp.where(jnp.asarray(same_head & (d_new >= 0)), b_new, MASKED)

    n_steps = n_pages // pps
    kernel = functools.partial(_attn_sample_kernel, scale=hd ** -0.5, n_heads=n_heads, tn=tn,
                               pages_per_block=ppb)
    page_spec = lambda s: pl.BlockSpec((None, cols, hd), lambda bi, jj, pt: (pt[bi, jj * pps + s], 0, 0))
    per_b = lambda shape: pl.BlockSpec((None,) + shape, lambda bi, jj, pt: (bi, 0, 0))
    grid_spec = pltpu.PrefetchScalarGridSpec(
        num_scalar_prefetch=1,
        grid=(b, n_steps),
        in_specs=[per_b((rows, hd)), per_b((rows, hd))]
                 + [page_spec(s) for s in range(pps)] + [page_spec(s) for s in range(pps)]
                 + [pl.BlockSpec((None, rows, cols), lambda bi, jj, pt: (0, 0, 0)),
                    pl.BlockSpec((None, rows, cols), lambda bi, jj, pt: (jj // (n_steps - 1) if n_steps > 1 else 1, 0, 0)),
                    per_b((tn * n_heads, hd)), per_b((tn * n_heads, hd)),
                    pl.BlockSpec((rows, tn * n_heads), lambda bi, jj, pt: (0, 0)),
                    per_b((tn, width))],
        out_specs=per_b((tn, width)),
        scratch_shapes=[pltpu.VMEM((rows, LANES), _F32), pltpu.VMEM((rows, LANES), _F32),
                        pltpu.VMEM((n_pages, rows, hd), _F32), pltpu.VMEM((nblk, n_heads, hd), _F32)],
    )
    return pl.pallas_call(
        kernel,
        grid_spec=grid_spec,
        out_shape=jax.ShapeDtypeStruct((b, tn, width), _F32),
        compiler_params=_params("parallel", "arbitrary"),
        name="attn_sample",
    )(page_table, qh, qh.astype(_BF16), *([cache_k] * pps), *([cache_v] * pps),
      bias_pages, bias_pages, kn, vn, bias_new, rest).astype(_BF16)


def _conv_combine(u, prev1, prev2, w_ref, b_ref, g_ref, o_ref):
    y = w_ref[0:1, :] * prev2 + w_ref[1:2, :] * prev1 + w_ref[2:3, :] * u
    g = g_ref[...]
    o_ref[...] = (b_ref[...] * y * (g * jax.nn.sigmoid(g))).astype(o_ref.dtype)


def _conv_prompt_kernel(b_ref, c_ref, h_ref, g_ref, ch_ref, hh_ref, w_ref, o_ref, st_ref):
    i = pl.program_id(1)
    u = c_ref[...] * h_ref[...]
    halo = ch_ref[...] * hh_ref[...] * (i > 0).astype(_F32)
    row = lax.broadcasted_iota(jnp.int32, u.shape, 0)
    prev1 = jnp.where(row == 0, halo[7:8, :], pltpu.roll(u, 1, axis=0))
    prev2 = jnp.where(row == 0, halo[6:7, :], jnp.where(row == 1, halo[7:8, :], pltpu.roll(u, 2, axis=0)))
    _conv_combine(u, prev1, prev2, w_ref, b_ref, g_ref, o_ref)
    st_ref[...] = u[u.shape[0] - 8:, :]


def _conv_sample_kernel(b_ref, c_ref, h_ref, g_ref, s1_ref, s2_ref, w_ref, o_ref, u_ref, *, tn):
    u = c_ref[...] * h_ref[...]
    t = lax.broadcasted_iota(jnp.int32, u.shape, 0) % tn
    prev1 = jnp.where(t >= 1, pltpu.roll(u, 1, axis=0), s1_ref[...])
    prev2 = jnp.where(t >= 2, pltpu.roll(u, 2, axis=0), s2_ref[...])
    _conv_combine(u, prev1, prev2, w_ref, b_ref, g_ref, o_ref)
    u_ref[...] = u


def _rest_cols(cw, tc):
    return [(1 + n) * cw // tc for n in range(4)]


def _conv_prompt(rest, conv_w, cw):
    t = rest.shape[0]
    tm = _tile(t, 512)
    tc = _tile(cw, 512)
    ob, oc, oh, og = _rest_cols(cw, tc)
    main = lambda off: pl.BlockSpec((tm, tc), lambda j, i: (i, off + j))
    halo = lambda off: pl.BlockSpec((8, tc), lambda j, i: (jnp.maximum(i * (tm // 8) - 1, 0), off + j))
    return pl.pallas_call(
        _conv_prompt_kernel,
        grid=(cw // tc, t // tm),
        in_specs=[main(ob), main(oc), main(oh), main(og), halo(oc), halo(oh),
                  pl.BlockSpec((conv_w.shape[0], tc), lambda j, i: (0, j))],
        out_specs=[pl.BlockSpec((tm, tc), lambda j, i: (i, j)),
                   pl.BlockSpec((8, tc), lambda j, i: (0, j))],
        out_shape=[jax.ShapeDtypeStruct((t, cw), _BF16), jax.ShapeDtypeStruct((8, cw), _F32)],
        compiler_params=_params("parallel", "arbitrary"),
        name="conv_prompt",
    )(rest, rest, rest, rest, rest, rest, conv_w)


def _conv_sample(rest, state, conv_w, cw, tn):
    m = rest.shape[0]
    b = m // tn
    assert tn >= 2 and conv_w.shape[0] == 3
    tc = _tile(cw, 512)
    ob, oc, oh, og = _rest_cols(cw, tc)
    zeros = jnp.zeros((b, tn - 1, cw), _F32)
    s1 = jnp.concatenate([state[:, 1:2], zeros], axis=1).reshape(m, cw)
    s2 = jnp.concatenate([state, zeros[:, 1:]], axis=1).reshape(m, cw)
    main = lambda off: pl.BlockSpec((m, tc), lambda j: (0, off + j))
    own = pl.BlockSpec((m, tc), lambda j: (0, j))
    return pl.pallas_call(
        functools.partial(_conv_sample_kernel, tn=tn),
        grid=(cw // tc,),
        in_specs=[main(ob), main(oc), main(oh), main(og), own, own,
                  pl.BlockSpec((conv_w.shape[0], tc), lambda j: (0, j))],
        out_specs=[own, own],
        out_shape=[jax.ShapeDtypeStruct((m, cw), _BF16), jax.ShapeDtypeStruct((m, cw), _F32)],
        compiler_params=_params("parallel"),
        name="conv_sample",
    )(rest, rest, rest, rest, s1, s2, conv_w)


def _merge_kernel(a_ref, c_ref, wa_ref, wc_ref, ma_ref, mc_ref, o_ref):
    ya = jnp.dot(a_ref[...], wa_ref[...], preferred_element_type=_F32)
    yc = jnp.dot(c_ref[...], wc_ref[...], preferred_element_type=_F32)
    o_ref[...] = (jax.nn.sigmoid(ma_ref[...]) * ya + jax.nn.sigmoid(mc_ref[...]) * yc).astype(o_ref.dtype)


def _merge(a, c, wa, wc, rest, gate_col0):
    m, ka = a.shape
    kc = c.shape[1]
    d = wa.shape[1]
    tm = _tile(m, 512)
    tn = _tile(math.gcd(d, gate_col0), 1024)
    oa = gate_col0 // tn
    oc = (gate_col0 + d) // tn
    return pl.pallas_call(
        _merge_kernel,
        grid=(d // tn, m // tm),
        in_specs=[pl.BlockSpec((tm, ka), lambda j, i: (i, 0)),
                  pl.BlockSpec((tm, kc), lambda j, i: (i, 0)),
                  pl.BlockSpec((ka, tn), lambda j, i: (0, j)),
                  pl.BlockSpec((kc, tn), lambda j, i: (0, j)),
                  pl.BlockSpec((tm, tn), lambda j, i: (i, oa + j)),
                  pl.BlockSpec((tm, tn), lambda j, i: (i, oc + j))],
        out_specs=pl.BlockSpec((tm, tn), lambda j, i: (i, j)),
        out_shape=jax.ShapeDtypeStruct((m, d), _BF16),
        compiler_params=_params("parallel", "arbitrary"),
        name="merge",
    )(a, c, wa, wc, rest, rest)


def _out_kernel(mg_ref, w_ref, x_ref, g_ref, o_ref, *, tn, nj):
    j = pl.program_id(1)
    z = x_ref[...] + jnp.dot(mg_ref[...], w_ref[...], preferred_element_type=_F32)
    for jj in range(nj):
        @pl.when(j == jj)
        def _(jj=jj):
            o_ref[:, jj * tn:(jj + 1) * tn] = z

    @pl.when(j == nj - 1)
    def _():
        nr = math.gcd(o_ref.shape[0], NORM_ROWS)

        def norm_rows(r, _):
            rows = pl.ds(pl.multiple_of(r * nr, nr), nr)
            y = o_ref[rows, :]
            ms = jnp.mean(y * y, axis=-1, keepdims=True)
            o_ref[rows, :] = (y * lax.rsqrt(ms + EPS)) * g_ref[...]
            return 0

        lax.fori_loop(0, o_ref.shape[0] // nr, norm_rows, 0)


def _out(merged, w, x, gain):
    m, d = x.shape
    tm = _tile(m, 512)
    tn = _tile(d, 512)
    nj = d // tn
    return pl.pallas_call(
        functools.partial(_out_kernel, tn=tn, nj=nj),
        grid=(m // tm, nj),
        in_specs=[pl.BlockSpec((tm, d), lambda i, j: (i, 0)),
                  pl.BlockSpec((d, tn), lambda i, j: (0, j)),
                  pl.BlockSpec((tm, tn), lambda i, j: (i, j)),
                  pl.BlockSpec((1, d), lambda i, j: (0, 0))],
        out_specs=pl.BlockSpec((tm, d), lambda i, j: (i, 0)),
        out_shape=jax.ShapeDtypeStruct((m, d), _F32),
        compiler_params=_params("parallel", "arbitrary"),
        name="out_proj",
    )(merged, w, x, gain.reshape(1, d))


def _layer(x, gain, w_in, conv_w, wa, wc, w_out, out_gain, aw, cw, attend, conv):
    h = _rmsnorm_bf16(x, gain)
    q = _proj(h, w_in, 0, aw, "proj_q")
    k = _proj(h, w_in, aw, aw, "proj_k")
    v = _proj(h, w_in, 2 * aw, aw, "proj_v")
    rest = _proj(h, w_in, 3 * aw, w_in.shape[1] - 3 * aw, "proj_rest")
    a = attend(q, k, v, rest)
    c, state = conv(rest)
    merged = _merge(a, c, wa, wc, rest, aw + 4 * cw)
    return _out(merged, w_out, x, out_gain), k, v, state


def kernel(x_prompt, x_sample, cache_k, cache_v, state_conv, page_table, norm_gain, w_in,
           conv_w, w_attn_out, w_conv_out, w_out, rel_bias, final_gain):
    depth = w_in.shape[0]
    assert depth == 1, "final norm is fused into the (single) layer's output projection"
    bp, t, d = x_prompt.shape
    assert bp == 1
    db, tn, _ = x_sample.shape
    n_pool, page, n_heads, hd = cache_k.shape[1:]
    aw = n_heads * hd
    cw = conv_w.shape[2]
    assert aw == cw

    l = 0
    w_in_b = w_in[l].astype(_BF16)
    wa_b = w_attn_out[l].astype(_BF16)
    wc_b = w_conv_out[l].astype(_BF16)
    wo_b = w_out[l].astype(_BF16)
    ck = cache_k.reshape(depth * n_pool, page * n_heads, hd)
    cv = cache_v.reshape(depth * n_pool, page * n_heads, hd)

    def attend_prompt(q, k, v, rest):
        return _attn_prompt(q, k, v, rest, rel_bias, n_heads, hd)

    def conv_prompt(rest):
        c, st = _conv_prompt(rest, conv_w[l], cw)
        return c, st[8 - (conv_w.shape[1] - 1):]

    yp, kp, vp, cp = _layer(x_prompt.reshape(t, d), norm_gain[l], w_in_b, conv_w[l], wa_b, wc_b, wo_b,
                            final_gain, aw, cw, attend_prompt, conv_prompt)

    def attend_sample(q, k, v, rest):
        r3 = lambda z: z.reshape(db, tn, z.shape[-1])
        return _attn_sample(r3(q), r3(k), r3(v), r3(rest), ck, cv, page_table + l * n_pool, rel_bias,
                            n_heads, hd).reshape(db * tn, aw)

    def conv_sample(rest):
        c, u = _conv_sample(rest, state_conv[l], conv_w[l], cw, tn)
        return c, u.reshape(db, tn, cw)[:, tn - (conv_w.shape[1] - 1):]

    ys, ks, vs, cs = _layer(x_sample.reshape(db * tn, d), norm_gain[l], w_in_b, conv_w[l], wa_b, wc_b, wo_b,
                            final_gain, aw, cw, attend_sample, conv_sample)

    return (yp.reshape(1, t, d), ys.reshape(db, tn, d),
            kp.reshape(1, 1, t, n_heads, hd), vp.reshape(1, 1, t, n_heads, hd),
            cp.reshape(1, 1, conv_w.shape[1] - 1, cw),
            ks.reshape(1, db, tn, n_heads, hd), vs.reshape(1, db, tn, n_heads, hd),
            cs.reshape(1, db, conv_w.shape[1] - 1, cw))
```

```python
import functools
import math

import numpy as np
import jax
import jax.numpy as jnp
from jax import lax
from jax.experimental import pallas as pl
from jax.experimental.pallas import tpu as pltpu

MOBA_BLOCK = 256
MOBA_TOPK = 3
N_BUCKETS = 32
MAX_DISTANCE = 128
EPS = 1e-6
MASKED = -1e30
LANES = 128
NORM_ROWS = 64
LOG2E = math.log2(math.e)
FAR_GROUP = 2
HEADS_PER_STEP = 2
PAGES_PER_STEP = 8
VMEM_LIMIT_BYTES = 56 * 1024 * 1024

_BF16 = jnp.bfloat16
_F32 = jnp.float32


def _params(*semantics):
    return pltpu.CompilerParams(dimension_semantics=semantics, vmem_limit_bytes=VMEM_LIMIT_BYTES)


def _tile(dim, want):
    t = min(dim, want)
    while dim % t:
        t -= LANES
        assert t > 0, (dim, want)
    return t


def _rmsnorm_kernel(x_ref, g_ref, o_ref):
    x = x_ref[...]
    ms = jnp.mean(x * x, axis=-1, keepdims=True)
    o_ref[...] = ((x * lax.rsqrt(ms + EPS)) * g_ref[...]).astype(o_ref.dtype)


def _rmsnorm_bf16(x, gain):
    m, d = x.shape
    tm = _tile(m, 256)
    return pl.pallas_call(
        _rmsnorm_kernel,
        grid=(m // tm,),
        in_specs=[pl.BlockSpec((tm, d), lambda i: (i, 0)), pl.BlockSpec((1, d), lambda i: (0, 0))],
        out_specs=pl.BlockSpec((tm, d), lambda i: (i, 0)),
        out_shape=jax.ShapeDtypeStruct((m, d), _BF16),
        compiler_params=_params("parallel"),
        name="rmsnorm",
    )(x, gain.reshape(1, d))


def _proj_kernel(h_ref, w_ref, o_ref):
    o_ref[...] = jnp.dot(h_ref[...], w_ref[...], preferred_element_type=_F32).astype(o_ref.dtype)


def _proj(h, w, col0, ncols, name):
    m, k = h.shape
    tm = _tile(m, 1024)
    tn = _tile(math.gcd(ncols, col0), 1024)
    c0 = col0 // tn
    return pl.pallas_call(
        _proj_kernel,
        grid=(ncols // tn, m // tm),
        in_specs=[pl.BlockSpec((tm, k), lambda j, i: (i, 0)),
                  pl.BlockSpec((k, tn), lambda j, i: (0, c0 + j))],
        out_specs=pl.BlockSpec((tm, tn), lambda j, i: (i, j)),
        out_shape=jax.ShapeDtypeStruct((m, ncols), _F32),
        compiler_params=_params("parallel", "arbitrary"),
        name=name,
    )(h, w)


def _t5_bucket(dist):
    max_exact = N_BUCKETS // 2
    n = jnp.maximum(dist, 0)
    nf = jnp.maximum(n, 1).astype(_F32)
    large = max_exact + (jnp.log(nf / max_exact) / math.log(MAX_DISTANCE / max_exact)
                         * (N_BUCKETS - max_exact)).astype(jnp.int32)
    large = jnp.minimum(large, N_BUCKETS - 1)
    return jnp.where(n < max_exact, n, large)


def _bias_table(rel_bias, n_dist):
    return rel_bias[_t5_bucket(jnp.arange(n_dist, dtype=jnp.int32))].T.astype(_F32)


def _toeplitz(w):
    h, two_n = w.shape
    n = two_n // 2
    flat = jnp.tile(w, (1, n))[:, :n * (two_n - 1)]
    return flat.reshape(h, n, two_n - 1)[:, :, :n]


def _topk_mask(gate, n_valid, axis):
    idx = lax.broadcasted_iota(jnp.int32, gate.shape, axis).astype(_F32)
    g = jnp.where(idx < n_valid, gate, -jnp.inf)
    sel = jnp.zeros(gate.shape, _F32)
    for _ in range(MOBA_TOPK):
        mx = jnp.max(g, axis=axis, keepdims=True)
        first = jnp.min(jnp.where(g == mx, idx, float(gate.shape[axis])), axis=axis, keepdims=True)
        pick = idx == jnp.where(mx > -jnp.inf, first, -1.0)
        sel = jnp.where(pick, 1.0, sel)
        g = jnp.where(pick, -jnp.inf, g)
    return sel


def _nt_dot(a, b, precision=None):
    return lax.dot_general(a, b, (((1,), (1,)), ((), ())), precision=precision, preferred_element_type=_F32)


def _attn_prompt_kernel(far_ref, q_ref, k_ref, v_ref, g_ref, bown_ref, bprev_ref, o_ref,
                        kb_ref, vt_ref, km_ref, sel_ref, s_ref, *, qscale):
    hp = pl.program_id(0)
    i = pl.program_id(1)
    n_h, nb, blk, hd = kb_ref.shape
    head_cols = [slice(e * hd, (e + 1) * hd) for e in range(n_h)]

    @pl.when(i == 0)
    def _():
        def load_block(c, _):
            rows = pl.ds(pl.multiple_of(c * blk, blk), blk)
            for e in range(n_h):
                kf = k_ref[rows, head_cols[e]]
                kb_ref[e, c] = kf.astype(_BF16)
                vt_ref[e, c] = v_ref[rows, head_cols[e]].T.astype(_BF16)
                km_ref[e, pl.ds(c, 1), :] = jnp.mean(kf, axis=0, keepdims=True)
            return 0

        lax.fori_loop(0, nb, load_block, 0)

    def scores(e, j, qtb):
        return jnp.dot(kb_ref[e, j], qtb, preferred_element_type=_F32)

    def weighted_values(e, j, p):
        return jnp.dot(vt_ref[e, j], p.astype(_BF16), preferred_element_type=_F32)

    def far_block(gidx, g):
        return jnp.minimum(gidx * FAR_GROUP + g, nb - 1)

    def far_scores(buf, gidx):
        for e in range(n_h):
            for g in range(FAR_GROUP):
                jc = far_block(gidx, g)
                visible = gidx * FAR_GROUP + g < i - 1
                pen = jnp.where(sel_ref[e, pl.ds(jc, 1), :] > 0.0,
                                jnp.where(visible, far_biases[e], MASKED), MASKED)
                s_ref[buf, e, g] = scores(e, jc, qtbs[e]) + pen

    def far_softmax(buf, gidx, carry):
        out = []
        for e, (m, l, acc) in enumerate(carry):
            m_new = m
            for g in range(FAR_GROUP):
                m_new = jnp.maximum(m_new, jnp.max(s_ref[buf, e, g], axis=0, keepdims=True))
            alpha = jnp.exp2(m - m_new)
            l = alpha * l
            acc = alpha * acc
            for g in range(FAR_GROUP):
                p = jnp.exp2(s_ref[buf, e, g] - m_new)
                l = l + jnp.sum(p, axis=0, keepdims=True)
                acc = acc + weighted_values(e, far_block(gidx, g), p)
            out.append((m_new, l, acc))
        return tuple(out)

    jp = jnp.maximum(i - 1, 0)
    qtbs, far_biases, carry = [], [], []
    for e in range(n_h):
        qt = q_ref[:, head_cols[e]].T
        qtb = (qt * qscale).astype(_BF16)
        gate = jnp.dot(km_ref[e], qt, precision=lax.Precision.HIGHEST, preferred_element_type=_F32)
        sel_ref[e] = _topk_mask(gate, i.astype(_F32), axis=0)
        far_bias = far_ref[hp * n_h + e]
        pen_prev = jnp.where(sel_ref[e, pl.ds(jp, 1), :] > 0.0, 0.0, MASKED)
        s_own = scores(e, i, qtb) + bown_ref[e]
        s_prev = scores(e, jp, qtb) + bprev_ref[e] + pen_prev
        m = jnp.maximum(jnp.max(s_own, axis=0, keepdims=True), jnp.max(s_prev, axis=0, keepdims=True))
        p_own = jnp.exp2(s_own - m)
        p_prev = jnp.exp2(s_prev - m)
        l = jnp.sum(p_own, axis=0, keepdims=True) + jnp.sum(p_prev, axis=0, keepdims=True)
        acc = weighted_values(e, i, p_own) + weighted_values(e, jp, p_prev)
        qtbs.append(qtb)
        far_biases.append(far_bias)
        carry.append((m, l, acc))

    def far_pair(t, carry):
        far_scores(1, 2 * t + 1)
        carry = far_softmax(0, 2 * t, carry)
        far_scores(0, 2 * t + 2)
        return far_softmax(1, 2 * t + 1, carry)

    n_groups = (jnp.maximum(i - 1, 0) + FAR_GROUP - 1) // FAR_GROUP
    far_scores(0, 0)
    carry = lax.fori_loop(0, (n_groups + 1) // 2, far_pair, tuple(carry))
    for e, (m, l, acc) in enumerate(carry):
        g = g_ref[:, head_cols[e]]
        o_ref[:, head_cols[e]] = ((acc / l).T * (g * jax.nn.sigmoid(g))).astype(o_ref.dtype)


def _attn_prompt(q, k, v, rest, rel_bias, n_heads, hd):
    t = q.shape[0]
    blk = MOBA_BLOCK
    assert t % blk == 0 and hd == LANES
    nb = t // blk
    assert blk + 1 >= MAX_DISTANCE
    tab = _bias_table(rel_bias, 2 * blk)
    bown = _toeplitz(jnp.concatenate([tab[:, :blk], jnp.full((n_heads, blk), MASKED, _F32)], axis=1)) * LOG2E
    bprev = _toeplitz(jnp.concatenate([tab[:, blk:], tab[:, :blk]], axis=1)) * LOG2E
    far = rel_bias[N_BUCKETS - 1].astype(_F32) * LOG2E
    n_h = math.gcd(n_heads, HEADS_PER_STEP)
    kernel = functools.partial(_attn_prompt_kernel, qscale=hd ** -0.5 * LOG2E)
    return pl.pallas_call(
        kernel,
        grid=(n_heads // n_h, nb),
        in_specs=[pl.BlockSpec(memory_space=pltpu.SMEM),
                  pl.BlockSpec((blk, n_h * hd), lambda h, i: (i, h)),
                  pl.BlockSpec((t, n_h * hd), lambda h, i: (0, h)),
                  pl.BlockSpec((t, n_h * hd), lambda h, i: (0, h)),
                  pl.BlockSpec((blk, n_h * hd), lambda h, i: (i, h)),
                  pl.BlockSpec((n_h, blk, blk), lambda h, i: (h, 0, 0)),
                  pl.BlockSpec((n_h, blk, blk), lambda h, i: (h, 0, 0))],
        out_specs=pl.BlockSpec((blk, n_h * hd), lambda h, i: (i, h)),
        out_shape=jax.ShapeDtypeStruct((t, n_heads * hd), _BF16),
        scratch_shapes=[pltpu.VMEM((n_h, nb, blk, hd), _BF16), pltpu.VMEM((n_h, nb, hd, blk), _BF16),
                        pltpu.VMEM((n_h, nb, hd), _F32), pltpu.VMEM((n_h, nb, blk), _F32),
                        pltpu.VMEM((2, n_h, FAR_GROUP, blk, blk), _F32)],
        compiler_params=_params("parallel", "arbitrary"),
        name="attn_prompt",
    )(far, q, k, v, rest, bown, bprev)


def _attn_sample_kernel(pt_ref, q_ref, qb_ref, *refs, n_heads, tn, pages_per_block):
    pps = (len(refs) - 11) // 2
    k_refs, v_refs = refs[:pps], refs[pps:2 * pps]
    (bfar_ref, blast_ref, kn_ref, vn_ref, bnew_ref, g_ref, o_ref, m_ref, l_ref, pv_ref, ks_ref) = refs[2 * pps:]
    del pt_ref
    jj = pl.program_id(1)
    n_pages, rows, hd = pv_ref.shape
    nblk = ks_ref.shape[0] // n_heads
    n_steps = n_pages // pps
    qb = qb_ref[...]

    for s in range(pps):
        page = jj * pps + s
        kf = k_refs[s][...]
        bias = (blast_ref if s == pps - 1 else bfar_ref)[...]
        sc = _nt_dot(qb, kf.astype(_BF16)) + bias
        mp = jnp.max(sc, axis=-1, keepdims=True)
        p = jnp.exp2(sc - mp)
        lp = jnp.sum(p, axis=-1, keepdims=True)
        pv_ref[page] = jnp.dot(p.astype(_BF16), v_refs[s][...].astype(_BF16), preferred_element_type=_F32)
        m_ref[page] = jnp.broadcast_to(mp, (rows, hd))
        l_ref[page] = jnp.broadcast_to(lp, (rows, hd))
        ksum = jnp.sum(kf.reshape(kf.shape[0] // n_heads, n_heads, hd), axis=0)
        blk_rows = pl.ds(pl.multiple_of((page // pages_per_block) * n_heads, n_heads), n_heads)
        if s % pages_per_block == 0:
            ks_ref[blk_rows, :] = ksum
        else:
            ks_ref[blk_rows, :] = ks_ref[blk_rows, :] + ksum

    @pl.when(jj == n_steps - 1)
    def _():
        q32 = q_ref[...]
        g2 = _nt_dot(q32, ks_ref[...], precision=lax.Precision.HIGHEST)
        row_head = lax.broadcasted_iota(jnp.int32, g2.shape, 0) // tn
        col_head = lax.broadcasted_iota(jnp.int32, g2.shape, 1) % n_heads
        own_head = jnp.where(col_head == row_head, g2, 0.0)
        fold = (lax.broadcasted_iota(jnp.int32, (g2.shape[1], LANES), 0) // n_heads
                == lax.broadcasted_iota(jnp.int32, (g2.shape[1], LANES), 1)).astype(_F32)
        gate = jnp.dot(own_head, fold, precision=lax.Precision.HIGHEST, preferred_element_type=_F32)
        sel = _topk_mask(gate, float(nblk), axis=1)

        s_new = _nt_dot(qb, kn_ref[...]) + bnew_ref[...]
        m_new = jnp.max(s_new, axis=-1, keepdims=True)
        live = [jnp.broadcast_to(sel[:, b:b + 1], (rows, hd)) > 0.0 for b in range(nblk)]
        m_tot = jnp.broadcast_to(m_new, (rows, hd))
        for pg in range(n_pages):
            m_tot = jnp.maximum(m_tot, jnp.where(live[pg // pages_per_block], m_ref[pg], MASKED))
        p_new = jnp.exp2(s_new - m_tot[:, :s_new.shape[1]])
        l = jnp.broadcast_to(jnp.sum(p_new, axis=-1, keepdims=True), (rows, hd))
        acc = jnp.dot(p_new.astype(_BF16), vn_ref[...], preferred_element_type=_F32)
        for pg in range(n_pages):
            w = jnp.where(live[pg // pages_per_block], jnp.exp2(m_ref[pg] - m_tot), 0.0)
            l = l + w * l_ref[pg]
            acc = acc + w * pv_ref[pg]
        out = acc / l
        g = g_ref[...]
        gated = g * jax.nn.sigmoid(g)
        for hh in range(n_heads):
            cols = slice(hh * hd, (hh + 1) * hd)
            o_ref[:, cols] = (out[hh * tn:(hh + 1) * tn, :] * gated[:, cols]).astype(o_ref.dtype)


def _attn_sample(q, k_new, v_new, rest, cache_k, cache_v, page_table, rel_bias, n_heads, hd):
    b, tn, width = q.shape
    n_pages = page_table.shape[1]
    page = cache_k.shape[1] // n_heads
    blk = MOBA_BLOCK
    pps = math.gcd(n_pages, PAGES_PER_STEP)
    past = n_pages * page
    assert blk % page == 0 and past % blk == 0 and n_pages <= LANES
    assert hd == LANES and page + 1 >= MAX_DISTANCE
    ppb = blk // page
    assert pps % ppb == 0
    nblk = past // blk
    rows = n_heads * tn
    cols = page * n_heads

    qh = q.reshape(b, tn, n_heads, hd).transpose(0, 2, 1, 3).reshape(b, rows, hd)
    kn = k_new.reshape(b, tn * n_heads, hd).astype(_BF16)
    vn = v_new.reshape(b, tn * n_heads, hd).astype(_BF16)

    head_of_row = np.repeat(np.arange(n_heads), tn)
    t_of_row = np.tile(np.arange(tn), n_heads)
    diag = jnp.asarray(head_of_row[:, None] == np.tile(np.arange(n_heads), page)[None, :])
    tab = _bias_table(rel_bias, page + tn)
    far = rel_bias[N_BUCKETS - 1].astype(_F32)
    bias_far = jnp.where(diag, far[head_of_row][:, None], MASKED)
    d_last = page + np.arange(tn)[:, None] - np.arange(page)[None, :]
    b_last = tab[:, d_last]
    b_last = jnp.broadcast_to(b_last[:, :, :, None], (n_heads, tn, page, n_heads)).reshape(rows, cols)
    bias_pages = jnp.stack([bias_far, jnp.where(diag, b_last, MASKED)]) * LOG2E
    d_new = t_of_row[:, None] - np.repeat(np.arange(tn), n_heads)[None, :]
    same_head = head_of_row[:, None] == np.tile(np.arange(n_heads), tn)[None, :]
    b_new = tab[head_of_row[:, None], np.maximum(d_new, 0)]
    bias_new = jnp.where(jnp.asarray(same_head & (d_new >= 0)), b_new, MASKED) * LOG2E
    qb = (qh * (hd ** -0.5 * LOG2E)).astype(_BF16)

    n_steps = n_pages // pps
    kernel = functools.partial(_attn_sample_kernel, n_heads=n_heads, tn=tn,
                               pages_per_block=ppb)
    page_spec = lambda s: pl.BlockSpec((None, cols, hd), lambda bi, jj, pt: (pt[bi, jj * pps + s], 0, 0))
    per_b = lambda shape: pl.BlockSpec((None,) + shape, lambda bi, jj, pt: (bi, 0, 0))
    grid_spec = pltpu.PrefetchScalarGridSpec(
        num_scalar_prefetch=1,
        grid=(b, n_steps),
        in_specs=[per_b((rows, hd)), per_b((rows, hd))]
                 + [page_spec(s) for s in range(pps)] + [page_spec(s) for s in range(pps)]
                 + [pl.BlockSpec((None, rows, cols), lambda bi, jj, pt: (0, 0, 0)),
                    pl.BlockSpec((None, rows, cols), lambda bi, jj, pt: (jj // (n_steps - 1) if n_steps > 1 else 1, 0, 0)),
                    per_b((tn * n_heads, hd)), per_b((tn * n_heads, hd)),
                    pl.BlockSpec((rows, tn * n_heads), lambda bi, jj, pt: (0, 0)),
                    per_b((tn, width))],
        out_specs=per_b((tn, width)),
        scratch_shapes=[pltpu.VMEM((n_pages, rows, hd), _F32), pltpu.VMEM((n_pages, rows, hd), _F32),
                        pltpu.VMEM((n_pages, rows, hd), _F32), pltpu.VMEM((nblk * n_heads, hd), _F32)],
    )
    return pl.pallas_call(
        kernel,
        grid_spec=grid_spec,
        out_shape=jax.ShapeDtypeStruct((b, tn, width), _F32),
        compiler_params=_params("parallel", "arbitrary"),
        name="attn_sample",
    )(page_table, qh, qb, *([cache_k] * pps), *([cache_v] * pps),
      bias_pages, bias_pages, kn, vn, bias_new, rest).astype(_BF16)


def _conv_combine(u, prev1, prev2, w_ref, b_ref, g_ref, o_ref):
    y = w_ref[0:1, :] * prev2 + w_ref[1:2, :] * prev1 + w_ref[2:3, :] * u
    g = g_ref[...]
    o_ref[...] = (b_ref[...] * y * (g * jax.nn.sigmoid(g))).astype(o_ref.dtype)


def _conv_prompt_kernel(b_ref, c_ref, h_ref, g_ref, ch_ref, hh_ref, w_ref, o_ref, st_ref):
    i = pl.program_id(1)
    u = c_ref[...] * h_ref[...]
    halo = ch_ref[...] * hh_ref[...] * (i > 0).astype(_F32)
    row = lax.broadcasted_iota(jnp.int32, u.shape, 0)
    prev1 = jnp.where(row == 0, halo[7:8, :], pltpu.roll(u, 1, axis=0))
    prev2 = jnp.where(row == 0, halo[6:7, :], jnp.where(row == 1, halo[7:8, :], pltpu.roll(u, 2, axis=0)))
    _conv_combine(u, prev1, prev2, w_ref, b_ref, g_ref, o_ref)
    st_ref[...] = u[u.shape[0] - 8:, :]


def _conv_sample_kernel(b_ref, c_ref, h_ref, g_ref, s1_ref, s2_ref, w_ref, o_ref, u_ref, *, tn):
    u = c_ref[...] * h_ref[...]
    t = lax.broadcasted_iota(jnp.int32, u.shape, 0) % tn
    prev1 = jnp.where(t >= 1, pltpu.roll(u, 1, axis=0), s1_ref[...])
    prev2 = jnp.where(t >= 2, pltpu.roll(u, 2, axis=0), s2_ref[...])
    _conv_combine(u, prev1, prev2, w_ref, b_ref, g_ref, o_ref)
    u_ref[...] = u


def _rest_cols(cw, tc):
    return [(1 + n) * cw // tc for n in range(4)]


def _conv_prompt(rest, conv_w, cw):
    t = rest.shape[0]
    tm = _tile(t, 512)
    tc = _tile(cw, 512)
    ob, oc, oh, og = _rest_cols(cw, tc)
    main = lambda off: pl.BlockSpec((tm, tc), lambda j, i: (i, off + j))
    halo = lambda off: pl.BlockSpec((8, tc), lambda j, i: (jnp.maximum(i * (tm // 8) - 1, 0), off + j))
    return pl.pallas_call(
        _conv_prompt_kernel,
        grid=(cw // tc, t // tm),
        in_specs=[main(ob), main(oc), main(oh), main(og), halo(oc), halo(oh),
                  pl.BlockSpec((conv_w.shape[0], tc), lambda j, i: (0, j))],
        out_specs=[pl.BlockSpec((tm, tc), lambda j, i: (i, j)),
                   pl.BlockSpec((8, tc), lambda j, i: (0, j))],
        out_shape=[jax.ShapeDtypeStruct((t, cw), _BF16), jax.ShapeDtypeStruct((8, cw), _F32)],
        compiler_params=_params("parallel", "arbitrary"),
        name="conv_prompt",
    )(rest, rest, rest, rest, rest, rest, conv_w)


def _conv_sample(rest, state, conv_w, cw, tn):
    m = rest.shape[0]
    b = m // tn
    assert tn >= 2 and conv_w.shape[0] == 3
    tc = _tile(cw, 512)
    ob, oc, oh, og = _rest_cols(cw, tc)
    zeros = jnp.zeros((b, tn - 1, cw), _F32)
    s1 = jnp.concatenate([state[:, 1:2], zeros], axis=1).reshape(m, cw)
    s2 = jnp.concatenate([state, zeros[:, 1:]], axis=1).reshape(m, cw)
    main = lambda off: pl.BlockSpec((m, tc), lambda j: (0, off + j))
    own = pl.BlockSpec((m, tc), lambda j: (0, j))
    return pl.pallas_call(
        functools.partial(_conv_sample_kernel, tn=tn),
        grid=(cw // tc,),
        in_specs=[main(ob), main(oc), main(oh), main(og), own, own,
                  pl.BlockSpec((conv_w.shape[0], tc), lambda j: (0, j))],
        out_specs=[own, own],
        out_shape=[jax.ShapeDtypeStruct((m, cw), _BF16), jax.ShapeDtypeStruct((m, cw), _F32)],
        compiler_params=_params("parallel"),
        name="conv_sample",
    )(rest, rest, rest, rest, s1, s2, conv_w)


def _merge_kernel(a_ref, c_ref, wa_ref, wc_ref, ma_ref, mc_ref, o_ref):
    ya = jnp.dot(a_ref[...], wa_ref[...], preferred_element_type=_F32)
    yc = jnp.dot(c_ref[...], wc_ref[...], preferred_element_type=_F32)
    o_ref[...] = (jax.nn.sigmoid(ma_ref[...]) * ya + jax.nn.sigmoid(mc_ref[...]) * yc).astype(o_ref.dtype)


def _merge(a, c, wa, wc, rest, gate_col0):
    m, ka = a.shape
    kc = c.shape[1]
    d = wa.shape[1]
    tm = _tile(m, 512)
    tn = _tile(math.gcd(d, gate_col0), 1024)
    oa = gate_col0 // tn
    oc = (gate_col0 + d) // tn
    return pl.pallas_call(
        _merge_kernel,
        grid=(d // tn, m // tm),
        in_specs=[pl.BlockSpec((tm, ka), lambda j, i: (i, 0)),
                  pl.BlockSpec((tm, kc), lambda j, i: (i, 0)),
                  pl.BlockSpec((ka, tn), lambda j, i: (0, j)),
                  pl.BlockSpec((kc, tn), lambda j, i: (0, j)),
                  pl.BlockSpec((tm, tn), lambda j, i: (i, oa + j)),
                  pl.BlockSpec((tm, tn), lambda j, i: (i, oc + j))],
        out_specs=pl.BlockSpec((tm, tn), lambda j, i: (i, j)),
        out_shape=jax.ShapeDtypeStruct((m, d), _BF16),
        compiler_params=_params("parallel", "arbitrary"),
        name="merge",
    )(a, c, wa, wc, rest, rest)


def _out_kernel(mg_ref, w_ref, x_ref, g_ref, o_ref, *, tn, nj):
    j = pl.program_id(1)
    z = x_ref[...] + jnp.dot(mg_ref[...], w_ref[...], preferred_element_type=_F32)
    for jj in range(nj):
        @pl.when(j == jj)
        def _(jj=jj):
            o_ref[:, jj * tn:(jj + 1) * tn] = z

    @pl.when(j == nj - 1)
    def _():
        nr = math.gcd(o_ref.shape[0], NORM_ROWS)

        def norm_rows(r, _):
            rows = pl.ds(pl.multiple_of(r * nr, nr), nr)
            y = o_ref[rows, :]
            ms = jnp.mean(y * y, axis=-1, keepdims=True)
            o_ref[rows, :] = (y * lax.rsqrt(ms + EPS)) * g_ref[...]
            return 0

        lax.fori_loop(0, o_ref.shape[0] // nr, norm_rows, 0)


def _out(merged, w, x, gain):
    m, d = x.shape
    tm = _tile(m, 512)
    tn = _tile(d, 1024)
    nj = d // tn
    return pl.pallas_call(
        functools.partial(_out_kernel, tn=tn, nj=nj),
        grid=(m // tm, nj),
        in_specs=[pl.BlockSpec((tm, d), lambda i, j: (i, 0)),
                  pl.BlockSpec((d, tn), lambda i, j: (0, j)),
                  pl.BlockSpec((tm, tn), lambda i, j: (i, j)),
                  pl.BlockSpec((1, d), lambda i, j: (0, 0))],
        out_specs=pl.BlockSpec((tm, d), lambda i, j: (i, 0)),
        out_shape=jax.ShapeDtypeStruct((m, d), _F32),
        compiler_params=_params("parallel", "arbitrary"),
        name="out_proj",
    )(merged, w, x, gain.reshape(1, d))


def _layer(x, gain, w_in, conv_w, wa, wc, w_out, out_gain, aw, cw, attend, conv):
    h = _rmsnorm_bf16(x, gain)
    q = _proj(h, w_in, 0, aw, "proj_q")
    k = _proj(h, w_in, aw, aw, "proj_k")
    v = _proj(h, w_in, 2 * aw, aw, "proj_v")
    rest = _proj(h, w_in, 3 * aw, w_in.shape[1] - 3 * aw, "proj_rest")
    a = attend(q, k, v, rest)
    c, state = conv(rest)
    merged = _merge(a, c, wa, wc, rest, aw + 4 * cw)
    return _out(merged, w_out, x, out_gain), k, v, state


def kernel(x_prompt, x_sample, cache_k, cache_v, state_conv, page_table, norm_gain, w_in,
           conv_w, w_attn_out, w_conv_out, w_out, rel_bias, final_gain):
    depth = w_in.shape[0]
    assert depth == 1, "final norm is fused into the (single) layer's output projection"
    bp, t, d = x_prompt.shape
    assert bp == 1
    db, tn, _ = x_sample.shape
    n_pool, page, n_heads, hd = cache_k.shape[1:]
    aw = n_heads * hd
    cw = conv_w.shape[2]
    assert aw == cw

    l = 0
    w_in_b = w_in[l].astype(_BF16)
    wa_b = w_attn_out[l].astype(_BF16)
    wc_b = w_conv_out[l].astype(_BF16)
    wo_b = w_out[l].astype(_BF16)
    ck = cache_k.reshape(depth * n_pool, page * n_heads, hd)
    cv = cache_v.reshape(depth * n_pool, page * n_heads, hd)

    def attend_prompt(q, k, v, rest):
        return _attn_prompt(q, k, v, rest, rel_bias, n_heads, hd)

    def conv_prompt(rest):
        c, st = _conv_prompt(rest, conv_w[l], cw)
        return c, st[8 - (conv_w.shape[1] - 1):]

    yp, kp, vp, cp = _layer(x_prompt.reshape(t, d), norm_gain[l], w_in_b, conv_w[l], wa_b, wc_b, wo_b,
                            final_gain, aw, cw, attend_prompt, conv_prompt)

    def attend_sample(q, k, v, rest):
        r3 = lambda z: z.reshape(db, tn, z.shape[-1])
        return _attn_sample(r3(q), r3(k), r3(v), r3(rest), ck, cv, page_table + l * n_pool, rel_bias,
                            n_heads, hd).reshape(db * tn, aw)

    def conv_sample(rest):
        c, u = _conv_sample(rest, state_conv[l], conv_w[l], cw, tn)
        return c, u.reshape(db, tn, cw)[:, tn - (conv_w.shape[1] - 1):]

    ys, ks, vs, cs = _layer(x_sample.reshape(db * tn, d), norm_gain[l], w_in_b, conv_w[l], wa_b, wc_b, wo_b,
                            final_gain, aw, cw, attend_sample, conv_sample)

    return (yp.reshape(1, t, d), ys.reshape(db, tn, d),
            kp.reshape(1, 1, t, n_heads, hd), vp.reshape(1, 1, t, n_heads, hd),
            cp.reshape(1, 1, conv_w.shape[1] - 1, cw),
            ks.reshape(1, db, tn, n_heads, hd), vs.reshape(1, db, tn, n_heads, hd),
            cs.reshape(1, db, conv_w.shape[1] - 1, cw))
```

```python
import functools
import math

import numpy as np
import jax
import jax.numpy as jnp
from jax import lax
from jax.experimental import pallas as pl
from jax.experimental.pallas import tpu as pltpu

MOBA_BLOCK = 256
MOBA_TOPK = 3
N_BUCKETS = 32
MAX_DISTANCE = 128
EPS = 1e-6
MASKED = -1e30
LANES = 128
SUBLANES = 8
SUM_ROWS = 16
NORM_ROWS = 64
LOG2E = math.log2(math.e)
FAR_GROUP = 2
HEADS_PER_STEP = 2
PAGES_PER_STEP = 8
VMEM_LIMIT_BYTES = 56 * 1024 * 1024

_BF16 = jnp.bfloat16
_F32 = jnp.float32


def _params(*semantics):
    return pltpu.CompilerParams(dimension_semantics=semantics, vmem_limit_bytes=VMEM_LIMIT_BYTES)


def _tile(dim, want):
    t = min(dim, want)
    while dim % t:
        t -= LANES
        assert t > 0, (dim, want)
    return t


def _rmsnorm_kernel(x_ref, g_ref, o_ref):
    x = x_ref[...]
    ms = jnp.mean(x * x, axis=-1, keepdims=True)
    o_ref[...] = ((x * lax.rsqrt(ms + EPS)) * g_ref[...]).astype(o_ref.dtype)


def _rmsnorm_bf16(x, gain):
    m, d = x.shape
    tm = _tile(m, 256)
    return pl.pallas_call(
        _rmsnorm_kernel,
        grid=(m // tm,),
        in_specs=[pl.BlockSpec((tm, d), lambda i: (i, 0)), pl.BlockSpec((1, d), lambda i: (0, 0))],
        out_specs=pl.BlockSpec((tm, d), lambda i: (i, 0)),
        out_shape=jax.ShapeDtypeStruct((m, d), _BF16),
        compiler_params=_params("parallel"),
        name="rmsnorm",
    )(x, gain.reshape(1, d))


def _proj_kernel(h_ref, w_ref, o_ref):
    o_ref[...] = jnp.dot(h_ref[...], w_ref[...], preferred_element_type=_F32).astype(o_ref.dtype)


def _proj_cast_kernel(h_ref, w_ref, o_ref, wb_ref):
    @pl.when(pl.program_id(1) == 0)
    def _():
        wb_ref[...] = w_ref[...].astype(wb_ref.dtype)

    o_ref[...] = jnp.dot(h_ref[...], wb_ref[...], preferred_element_type=_F32).astype(o_ref.dtype)


def _proj(h, w, col0, ncols, name, cast=False):
    m, k = h.shape
    tm = _tile(m, 1024)
    tn = _tile(math.gcd(ncols, col0), 512 if cast else 1024)
    c0 = col0 // tn
    in_specs = [pl.BlockSpec((tm, k), lambda j, i: (i, 0)),
                pl.BlockSpec((k, tn), lambda j, i: (0, c0 + j))]
    out_spec = pl.BlockSpec((tm, tn), lambda j, i: (i, j))
    out_shape = jax.ShapeDtypeStruct((m, ncols), _F32)
    if cast:
        out_spec = [out_spec, pl.BlockSpec((k, tn), lambda j, i: (0, j))]
        out_shape = [out_shape, jax.ShapeDtypeStruct((k, ncols), _BF16)]
    return pl.pallas_call(
        _proj_cast_kernel if cast else _proj_kernel,
        grid=(ncols // tn, m // tm),
        in_specs=in_specs,
        out_specs=out_spec,
        out_shape=out_shape,
        compiler_params=_params("parallel", "arbitrary"),
        name=name,
    )(h, w)


def _t5_bucket(dist):
    max_exact = N_BUCKETS // 2
    n = jnp.maximum(dist, 0)
    nf = jnp.maximum(n, 1).astype(_F32)
    large = max_exact + (jnp.log(nf / max_exact) / math.log(MAX_DISTANCE / max_exact)
                         * (N_BUCKETS - max_exact)).astype(jnp.int32)
    large = jnp.minimum(large, N_BUCKETS - 1)
    return jnp.where(n < max_exact, n, large)


def _bias_table(rel_bias, n_dist):
    return rel_bias[_t5_bucket(jnp.arange(n_dist, dtype=jnp.int32))].T.astype(_F32)


def _toeplitz(w):
    h, two_n = w.shape
    n = two_n // 2
    flat = jnp.tile(w, (1, n))[:, :n * (two_n - 1)]
    return flat.reshape(h, n, two_n - 1)[:, :, :n]


def _topk_mask(gate, n_valid, axis):
    idx = lax.broadcasted_iota(jnp.int32, gate.shape, axis).astype(_F32)
    g = jnp.where(idx < n_valid, gate, -jnp.inf)
    sel = jnp.zeros(gate.shape, _F32)
    for _ in range(MOBA_TOPK):
        mx = jnp.max(g, axis=axis, keepdims=True)
        first = jnp.min(jnp.where(g == mx, idx, float(gate.shape[axis])), axis=axis, keepdims=True)
        pick = idx == jnp.where(mx > -jnp.inf, first, -1.0)
        sel = jnp.where(pick, 1.0, sel)
        g = jnp.where(pick, -jnp.inf, g)
    return sel


def _nt_dot(a, b, precision=None):
    return lax.dot_general(a, b, (((1,), (1,)), ((), ())), precision=precision, preferred_element_type=_F32)


def _attn_prompt_kernel(far_ref, q_ref, k_ref, v_ref, g_ref, bown_ref, bprev_ref, o_ref,
                        kb_ref, vt_ref, km_ref, sel_ref, s_ref, *, qscale):
    hp = pl.program_id(0)
    i = pl.program_id(1)
    n_h, nb, blk, hd = kb_ref.shape
    head_cols = [slice(e * hd, (e + 1) * hd) for e in range(n_h)]

    @pl.when(i == 0)
    def _():
        def load_block(c, _):
            rows = pl.ds(pl.multiple_of(c * blk, blk), blk)
            for e in range(n_h):
                kf = k_ref[rows, head_cols[e]]
                kb_ref[e, c] = kf.astype(_BF16)
                vt_ref[e, c, :hd, :] = v_ref[rows, head_cols[e]].T.astype(_BF16)
                extra = lax.broadcasted_iota(jnp.int32, (SUM_ROWS, blk), 0) == 0
                vt_ref[e, c, hd:, :] = jnp.where(extra, 1.0, 0.0).astype(_BF16)
                km_ref[e, pl.ds(c, 1), :] = jnp.mean(kf, axis=0, keepdims=True)
            return 0

        lax.fori_loop(0, nb, load_block, 0)

    def scores(e, j, qtb):
        return jnp.dot(kb_ref[e, j], qtb, preferred_element_type=_F32)

    def weighted_values(e, j, p):
        return jnp.dot(vt_ref[e, j], p.astype(_BF16), preferred_element_type=_F32)

    def far_block(gidx, g):
        return jnp.minimum(gidx * FAR_GROUP + g, nb - 1)

    def far_scores(buf, gidx):
        for e in range(n_h):
            for g in range(FAR_GROUP):
                s_ref[buf, e, g] = scores(e, far_block(gidx, g), qtbs[e])

    def far_bias_row(e, gidx, g):
        visible = gidx * FAR_GROUP + g < i - 1
        return jnp.where(sel_ref[e, pl.ds(far_block(gidx, g), 1), :] > 0.0,
                         jnp.where(visible, far_biases[e], MASKED), MASKED)

    def far_softmax(buf, gidx, carry):
        out = []
        for e, (m, acc) in enumerate(carry):
            pens = [far_bias_row(e, gidx, g) for g in range(FAR_GROUP)]
            m_new = m
            for g in range(FAR_GROUP):
                m_new = jnp.maximum(m_new, jnp.max(s_ref[buf, e, g], axis=0, keepdims=True) + pens[g])
            acc = jnp.exp2(m - m_new) * acc
            for g in range(FAR_GROUP):
                p = jnp.exp2(s_ref[buf, e, g] - (m_new - pens[g]))
                acc = acc + weighted_values(e, far_block(gidx, g), p)
            out.append((m_new, acc))
        return tuple(out)

    def split_bf16(x):
        hi = x.astype(_BF16)
        return hi, (x - hi.astype(_F32)).astype(_BF16)

    jp = jnp.maximum(i - 1, 0)
    far_biases = [far_ref[hp * n_h + e] for e in range(n_h)]
    qts = [q_ref[:, head_cols[e]].T for e in range(n_h)]
    qtbs = [(qt * qscale).astype(_BF16) for qt in qts]
    s_own = [scores(e, i, qtbs[e]) + bown_ref[e] for e in range(n_h)]
    gates = []
    for e in range(n_h):
        q_hi, q_lo = split_bf16(qts[e])
        km_hi, km_lo = split_bf16(km_ref[e])
        gates.append(jnp.dot(km_hi, q_hi, preferred_element_type=_F32)
                     + jnp.dot(km_lo, q_hi, preferred_element_type=_F32)
                     + jnp.dot(km_hi, q_lo, preferred_element_type=_F32))
    far_scores(0, 0)
    s_prev = [scores(e, jp, qtbs[e]) + bprev_ref[e] for e in range(n_h)]
    carry = []
    for e in range(n_h):
        m = jnp.max(s_own[e], axis=0, keepdims=True)
        acc = weighted_values(e, i, jnp.exp2(s_own[e] - m))
        sel_ref[e] = _topk_mask(gates[e], i.astype(_F32), axis=0)
        pen_prev = jnp.where(sel_ref[e, pl.ds(jp, 1), :] > 0.0, 0.0, MASKED)
        m_new = jnp.maximum(m, jnp.max(s_prev[e], axis=0, keepdims=True) + pen_prev)
        acc = jnp.exp2(m - m_new) * acc + weighted_values(e, jp, jnp.exp2(s_prev[e] - (m_new - pen_prev)))
        carry.append((m_new, acc))

    def far_pair(t, carry):
        far_scores(1, 2 * t + 1)
        carry = far_softmax(0, 2 * t, carry)
        far_scores(0, 2 * t + 2)
        return far_softmax(1, 2 * t + 1, carry)

    n_groups = (jnp.maximum(i - 1, 0) + FAR_GROUP - 1) // FAR_GROUP
    far_scores(0, 0)
    carry = lax.fori_loop(0, (n_groups + 1) // 2, far_pair, tuple(carry))
    for e, (m, acc) in enumerate(carry):
        g = g_ref[:, head_cols[e]]
        out = acc[:hd, :] / acc[hd:hd + 1, :]
        o_ref[:, head_cols[e]] = (out.T * (g * jax.nn.sigmoid(g))).astype(o_ref.dtype)


def _attn_prompt(q, k, v, rest, rel_bias, n_heads, hd):
    t = q.shape[0]
    blk = MOBA_BLOCK
    assert t % blk == 0 and hd == LANES
    nb = t // blk
    assert blk + 1 >= MAX_DISTANCE
    tab = _bias_table(rel_bias, 2 * blk)
    bown = _toeplitz(jnp.concatenate([tab[:, :blk], jnp.full((n_heads, blk), MASKED, _F32)], axis=1)) * LOG2E
    bprev = _toeplitz(jnp.concatenate([tab[:, blk:], tab[:, :blk]], axis=1)) * LOG2E
    far = rel_bias[N_BUCKETS - 1].astype(_F32) * LOG2E
    n_h = math.gcd(n_heads, HEADS_PER_STEP)
    kernel = functools.partial(_attn_prompt_kernel, qscale=hd ** -0.5 * LOG2E)
    return pl.pallas_call(
        kernel,
        grid=(n_heads // n_h, nb),
        in_specs=[pl.BlockSpec(memory_space=pltpu.SMEM),
                  pl.BlockSpec((blk, n_h * hd), lambda h, i: (i, h)),
                  pl.BlockSpec((t, n_h * hd), lambda h, i: (0, h)),
                  pl.BlockSpec((t, n_h * hd), lambda h, i: (0, h)),
                  pl.BlockSpec((blk, n_h * hd), lambda h, i: (i, h)),
                  pl.BlockSpec((n_h, blk, blk), lambda h, i: (h, 0, 0)),
                  pl.BlockSpec((n_h, blk, blk), lambda h, i: (h, 0, 0))],
        out_specs=pl.BlockSpec((blk, n_h * hd), lambda h, i: (i, h)),
        out_shape=jax.ShapeDtypeStruct((t, n_heads * hd), _BF16),
        scratch_shapes=[pltpu.VMEM((n_h, nb, blk, hd), _BF16), pltpu.VMEM((n_h, nb, hd + SUM_ROWS, blk), _BF16),
                        pltpu.VMEM((n_h, nb, hd), _F32), pltpu.VMEM((n_h, nb, blk), _F32),
                        pltpu.VMEM((2, n_h, FAR_GROUP, blk, blk), _F32)],
        compiler_params=_params("parallel", "arbitrary"),
        name="attn_prompt",
    )(far, q, k, v, rest, bown, bprev)


def _attn_sample_kernel(pt_ref, q_ref, qb_ref, *refs, n_heads, tn, pages_per_block):
    pps = (len(refs) - 11) // 2
    k_refs, v_refs = refs[:pps], refs[pps:2 * pps]
    (bfar_ref, blast_ref, kn_ref, vn_ref, bnew_ref, g_ref, o_ref, m_ref, l_ref, pv_ref, ks_ref) = refs[2 * pps:]
    del pt_ref
    jj = pl.program_id(1)
    n_pages, rows, hd = pv_ref.shape
    nblk = ks_ref.shape[0] // n_heads
    n_steps = n_pages // pps
    qb = qb_ref[...]

    for s in range(pps):
        page = jj * pps + s
        kf = k_refs[s][...]
        bias = (blast_ref if s == pps - 1 else bfar_ref)[...]
        sc = _nt_dot(qb, kf.astype(_BF16)) + bias
        mp = jnp.max(sc, axis=-1, keepdims=True)
        p = jnp.exp2(sc - mp)
        lp = jnp.sum(p, axis=-1, keepdims=True)
        pv_ref[page] = jnp.dot(p.astype(_BF16), v_refs[s][...].astype(_BF16), preferred_element_type=_F32)
        m_ref[page] = jnp.broadcast_to(mp, (rows, hd))
        l_ref[page] = jnp.broadcast_to(lp, (rows, hd))
        ksum = jnp.sum(kf.reshape(kf.shape[0] // n_heads, n_heads, hd), axis=0)
        blk_rows = pl.ds(pl.multiple_of((page // pages_per_block) * n_heads, n_heads), n_heads)
        if s % pages_per_block == 0:
            ks_ref[blk_rows, :] = ksum
        else:
            ks_ref[blk_rows, :] = ks_ref[blk_rows, :] + ksum

    @pl.when(jj == n_steps - 1)
    def _():
        q32 = q_ref[...]
        g2 = _nt_dot(q32, ks_ref[...], precision=lax.Precision.HIGHEST)
        row_head = lax.broadcasted_iota(jnp.int32, g2.shape, 0) // tn
        col_head = lax.broadcasted_iota(jnp.int32, g2.shape, 1) % n_heads
        own_head = jnp.where(col_head == row_head, g2, 0.0)
        fold = (lax.broadcasted_iota(jnp.int32, (g2.shape[1], LANES), 0) // n_heads
                == lax.broadcasted_iota(jnp.int32, (g2.shape[1], LANES), 1)).astype(_F32)
        gate = jnp.dot(own_head, fold, precision=lax.Precision.HIGHEST, preferred_element_type=_F32)
        sel = _topk_mask(gate, float(nblk), axis=1)

        s_new = _nt_dot(qb, kn_ref[...]) + bnew_ref[...]
        m_new = jnp.max(s_new, axis=-1, keepdims=True)
        live = [jnp.broadcast_to(sel[:, b:b + 1], (rows, hd)) > 0.0 for b in range(nblk)]
        m_tot = jnp.broadcast_to(m_new, (rows, hd))
        for pg in range(n_pages):
            m_tot = jnp.maximum(m_tot, jnp.where(live[pg // pages_per_block], m_ref[pg], MASKED))
        p_new = jnp.exp2(s_new - m_tot[:, :s_new.shape[1]])
        l = jnp.broadcast_to(jnp.sum(p_new, axis=-1, keepdims=True), (rows, hd))
        acc = jnp.dot(p_new.astype(_BF16), vn_ref[...], preferred_element_type=_F32)
        for pg in range(n_pages):
            w = jnp.where(live[pg // pages_per_block], jnp.exp2(m_ref[pg] - m_tot), 0.0)
            l = l + w * l_ref[pg]
            acc = acc + w * pv_ref[pg]
        out = acc / l
        for pos in range(SUBLANES // tn):
            @pl.when(pl.program_id(0) % (SUBLANES // tn) == pos)
            def _(pos=pos):
                mine = slice(pos * tn, (pos + 1) * tn)
                g = g_ref[mine, :]
                gated = g * jax.nn.sigmoid(g)
                for hh in range(n_heads):
                    cols = slice(hh * hd, (hh + 1) * hd)
                    o_ref[mine, cols] = (out[hh * tn:(hh + 1) * tn, :] * gated[:, cols]).astype(o_ref.dtype)


def _attn_sample(q, k_new, v_new, rest, cache_k, cache_v, page_table, rel_bias, n_heads, hd):
    b, tn, width = q.shape
    n_pages = page_table.shape[1]
    page = cache_k.shape[1] // n_heads
    blk = MOBA_BLOCK
    pps = math.gcd(n_pages, PAGES_PER_STEP)
    past = n_pages * page
    assert blk % page == 0 and past % blk == 0 and n_pages <= LANES
    assert hd == LANES and page + 1 >= MAX_DISTANCE
    ppb = blk // page
    assert pps % ppb == 0
    nblk = past // blk
    rows = n_heads * tn
    cols = page * n_heads

    qh = q.reshape(b, tn, n_heads, hd).transpose(0, 2, 1, 3).reshape(b, rows, hd)
    kn = k_new.reshape(b, tn * n_heads, hd).astype(_BF16)
    vn = v_new.reshape(b, tn * n_heads, hd).astype(_BF16)

    head_of_row = np.repeat(np.arange(n_heads), tn)
    t_of_row = np.tile(np.arange(tn), n_heads)
    diag = jnp.asarray(head_of_row[:, None] == np.tile(np.arange(n_heads), page)[None, :])
    tab = _bias_table(rel_bias, page + tn)
    far = rel_bias[N_BUCKETS - 1].astype(_F32)
    bias_far = jnp.where(diag, far[head_of_row][:, None], MASKED)
    d_last = page + np.arange(tn)[:, None] - np.arange(page)[None, :]
    b_last = tab[:, d_last]
    b_last = jnp.broadcast_to(b_last[:, :, :, None], (n_heads, tn, page, n_heads)).reshape(rows, cols)
    bias_pages = jnp.stack([bias_far, jnp.where(diag, b_last, MASKED)]) * LOG2E
    d_new = t_of_row[:, None] - np.repeat(np.arange(tn), n_heads)[None, :]
    same_head = head_of_row[:, None] == np.tile(np.arange(n_heads), tn)[None, :]
    b_new = tab[head_of_row[:, None], np.maximum(d_new, 0)]
    bias_new = jnp.where(jnp.asarray(same_head & (d_new >= 0)), b_new, MASKED) * LOG2E
    qb = (qh * (hd ** -0.5 * LOG2E)).astype(_BF16)

    n_steps = n_pages // pps
    kernel = functools.partial(_attn_sample_kernel, n_heads=n_heads, tn=tn,
                               pages_per_block=ppb)
    page_spec = lambda s: pl.BlockSpec((None, cols, hd), lambda bi, jj, pt: (pt[bi, jj * pps + s], 0, 0))
    per_b = lambda shape: pl.BlockSpec((None,) + shape, lambda bi, jj, pt: (bi, 0, 0))
    assert SUBLANES % tn == 0 and (b * tn) % SUBLANES == 0
    shared_rows = pl.BlockSpec((SUBLANES, width), lambda bi, jj, pt: (bi * tn // SUBLANES, 0))
    grid_spec = pltpu.PrefetchScalarGridSpec(
        num_scalar_prefetch=1,
        grid=(b, n_steps),
        in_specs=[per_b((rows, hd)), per_b((rows, hd))]
                 + [page_spec(s) for s in range(pps)] + [page_spec(s) for s in range(pps)]
                 + [pl.BlockSpec((None, rows, cols), lambda bi, jj, pt: (0, 0, 0)),
                    pl.BlockSpec((None, rows, cols), lambda bi, jj, pt: (jj // (n_steps - 1) if n_steps > 1 else 1, 0, 0)),
                    per_b((tn * n_heads, hd)), per_b((tn * n_heads, hd)),
                    pl.BlockSpec((rows, tn * n_heads), lambda bi, jj, pt: (0, 0)),
                    shared_rows],
        out_specs=shared_rows,
        scratch_shapes=[pltpu.VMEM((n_pages, rows, hd), _F32), pltpu.VMEM((n_pages, rows, hd), _F32),
                        pltpu.VMEM((n_pages, rows, hd), _F32), pltpu.VMEM((nblk * n_heads, hd), _F32)],
    )
    return pl.pallas_call(
        kernel,
        grid_spec=grid_spec,
        out_shape=jax.ShapeDtypeStruct((b * tn, width), _F32),
        compiler_params=_params("arbitrary", "arbitrary"),
        name="attn_sample",
    )(page_table, qh, qb, *([cache_k] * pps), *([cache_v] * pps),
      bias_pages, bias_pages, kn, vn, bias_new, rest).astype(_BF16)


def _conv_combine(u, prev1, prev2, w_ref, b_ref, g_ref, o_ref):
    y = w_ref[0:1, :] * prev2 + w_ref[1:2, :] * prev1 + w_ref[2:3, :] * u
    g = g_ref[...]
    o_ref[...] = (b_ref[...] * y * (g * jax.nn.sigmoid(g))).astype(o_ref.dtype)


def _conv_prompt_kernel(b_ref, c_ref, h_ref, g_ref, ch_ref, hh_ref, w_ref, o_ref, st_ref):
    i = pl.program_id(1)
    u = c_ref[...] * h_ref[...]
    halo = ch_ref[...] * hh_ref[...] * (i > 0).astype(_F32)
    row = lax.broadcasted_iota(jnp.int32, u.shape, 0)
    prev1 = jnp.where(row == 0, halo[7:8, :], pltpu.roll(u, 1, axis=0))
    prev2 = jnp.where(row == 0, halo[6:7, :], jnp.where(row == 1, halo[7:8, :], pltpu.roll(u, 2, axis=0)))
    _conv_combine(u, prev1, prev2, w_ref, b_ref, g_ref, o_ref)
    st_ref[...] = u[u.shape[0] - 8:, :]


def _conv_sample_kernel(b_ref, c_ref, h_ref, g_ref, s1_ref, s2_ref, w_ref, o_ref, u_ref, *, tn):
    u = c_ref[...] * h_ref[...]
    t = lax.broadcasted_iota(jnp.int32, u.shape, 0) % tn
    prev1 = jnp.where(t >= 1, pltpu.roll(u, 1, axis=0), s1_ref[...])
    prev2 = jnp.where(t >= 2, pltpu.roll(u, 2, axis=0), s2_ref[...])
    _conv_combine(u, prev1, prev2, w_ref, b_ref, g_ref, o_ref)
    u_ref[...] = u


def _rest_cols(cw, tc):
    return [(1 + n) * cw // tc for n in range(4)]


def _conv_prompt(rest, conv_w, cw):
    t = rest.shape[0]
    tm = _tile(t, 512)
    tc = _tile(cw, 512)
    ob, oc, oh, og = _rest_cols(cw, tc)
    main = lambda off: pl.BlockSpec((tm, tc), lambda j, i: (i, off + j))
    halo = lambda off: pl.BlockSpec((8, tc), lambda j, i: (jnp.maximum(i * (tm // 8) - 1, 0), off + j))
    return pl.pallas_call(
        _conv_prompt_kernel,
        grid=(cw // tc, t // tm),
        in_specs=[main(ob), main(oc), main(oh), main(og), halo(oc), halo(oh),
                  pl.BlockSpec((conv_w.shape[0], tc), lambda j, i: (0, j))],
        out_specs=[pl.BlockSpec((tm, tc), lambda j, i: (i, j)),
                   pl.BlockSpec((8, tc), lambda j, i: (0, j))],
        out_shape=[jax.ShapeDtypeStruct((t, cw), _BF16), jax.ShapeDtypeStruct((8, cw), _F32)],
        compiler_params=_params("parallel", "arbitrary"),
        name="conv_prompt",
    )(rest, rest, rest, rest, rest, rest, conv_w)


def _conv_sample(rest, state, conv_w, cw, tn):
    m = rest.shape[0]
    b = m // tn
    assert tn >= 2 and conv_w.shape[0] == 3
    tc = _tile(cw, 512)
    ob, oc, oh, og = _rest_cols(cw, tc)
    zeros = jnp.zeros((b, tn - 1, cw), _F32)
    s1 = jnp.concatenate([state[:, 1:2], zeros], axis=1).reshape(m, cw)
    s2 = jnp.concatenate([state, zeros[:, 1:]], axis=1).reshape(m, cw)
    main = lambda off: pl.BlockSpec((m, tc), lambda j: (0, off + j))
    own = pl.BlockSpec((m, tc), lambda j: (0, j))
    return pl.pallas_call(
        functools.partial(_conv_sample_kernel, tn=tn),
        grid=(cw // tc,),
        in_specs=[main(ob), main(oc), main(oh), main(og), own, own,
                  pl.BlockSpec((conv_w.shape[0], tc), lambda j: (0, j))],
        out_specs=[own, own],
        out_shape=[jax.ShapeDtypeStruct((m, cw), _BF16), jax.ShapeDtypeStruct((m, cw), _F32)],
        compiler_params=_params("parallel"),
        name="conv_sample",
    )(rest, rest, rest, rest, s1, s2, conv_w)


def _merge_kernel(a_ref, c_ref, wa_ref, wc_ref, ma_ref, mc_ref, o_ref):
    ya = jnp.dot(a_ref[...], wa_ref[...], preferred_element_type=_F32)
    yc = jnp.dot(c_ref[...], wc_ref[...], preferred_element_type=_F32)
    o_ref[...] = (jax.nn.sigmoid(ma_ref[...]) * ya + jax.nn.sigmoid(mc_ref[...]) * yc).astype(o_ref.dtype)


def _merge(a, c, wa, wc, rest, gate_col0):
    m, ka = a.shape
    kc = c.shape[1]
    d = wa.shape[1]
    tm = _tile(m, 512)
    tn = _tile(math.gcd(d, gate_col0), 1024)
    oa = gate_col0 // tn
    oc = (gate_col0 + d) // tn
    return pl.pallas_call(
        _merge_kernel,
        grid=(d // tn, m // tm),
        in_specs=[pl.BlockSpec((tm, ka), lambda j, i: (i, 0)),
                  pl.BlockSpec((tm, kc), lambda j, i: (i, 0)),
                  pl.BlockSpec((ka, tn), lambda j, i: (0, j)),
                  pl.BlockSpec((kc, tn), lambda j, i: (0, j)),
                  pl.BlockSpec((tm, tn), lambda j, i: (i, oa + j)),
                  pl.BlockSpec((tm, tn), lambda j, i: (i, oc + j))],
        out_specs=pl.BlockSpec((tm, tn), lambda j, i: (i, j)),
        out_shape=jax.ShapeDtypeStruct((m, d), _BF16),
        compiler_params=_params("parallel", "arbitrary"),
        name="merge",
    )(a, c, wa, wc, rest, rest)


def _out_kernel(mg_ref, w_ref, x_ref, g_ref, o_ref, *, tn, nj):
    j = pl.program_id(1)
    z = x_ref[...] + jnp.dot(mg_ref[...], w_ref[...], preferred_element_type=_F32)
    for jj in range(nj):
        @pl.when(j == jj)
        def _(jj=jj):
            o_ref[:, jj * tn:(jj + 1) * tn] = z

    @pl.when(j == nj - 1)
    def _():
        nr = math.gcd(o_ref.shape[0], NORM_ROWS)

        def norm_rows(r, _):
            rows = pl.ds(pl.multiple_of(r * nr, nr), nr)
            y = o_ref[rows, :]
            ms = jnp.mean(y * y, axis=-1, keepdims=True)
            o_ref[rows, :] = (y * lax.rsqrt(ms + EPS)) * g_ref[...]
            return 0

        lax.fori_loop(0, o_ref.shape[0] // nr, norm_rows, 0)


def _out(merged, w, x, gain):
    m, d = x.shape
    tm = _tile(m, 512)
    tn = _tile(d, 1024)
    nj = d // tn
    return pl.pallas_call(
        functools.partial(_out_kernel, tn=tn, nj=nj),
        grid=(m // tm, nj),
        in_specs=[pl.BlockSpec((tm, d), lambda i, j: (i, 0)),
                  pl.BlockSpec((d, tn), lambda i, j: (0, j)),
                  pl.BlockSpec((tm, tn), lambda i, j: (i, j)),
                  pl.BlockSpec((1, d), lambda i, j: (0, 0))],
        out_specs=pl.BlockSpec((tm, d), lambda i, j: (i, 0)),
        out_shape=jax.ShapeDtypeStruct((m, d), _F32),
        compiler_params=_params("parallel", "arbitrary"),
        name="out_proj",
    )(merged, w, x, gain.reshape(1, d))


def _layer(x, gain, w_in, conv_w, wa, wc, w_out, out_gain, aw, cw, attend, conv):
    h = _rmsnorm_bf16(x, gain)
    names = ("proj_q", "proj_k", "proj_v", "proj_rest")
    if isinstance(w_in, (list, tuple)):
        w_groups = list(w_in)
        q, k, v, rest = (_proj(h, w, 0, w.shape[1], name) for w, name in zip(w_groups, names))
    else:
        col0s = (0, aw, 2 * aw, 3 * aw)
        widths = (aw, aw, aw, w_in.shape[1] - 3 * aw)
        outs = [_proj(h, w_in, c0, n, name, cast=True) for c0, n, name in zip(col0s, widths, names)]
        (q, k, v, rest), w_groups = zip(*outs)
    a = attend(q, k, v, rest)
    c, state = conv(rest)
    merged = _merge(a, c, wa, wc, rest, aw + 4 * cw)
    return _out(merged, w_out, x, out_gain), k, v, state, list(w_groups)


def kernel(x_prompt, x_sample, cache_k, cache_v, state_conv, page_table, norm_gain, w_in,
           conv_w, w_attn_out, w_conv_out, w_out, rel_bias, final_gain):
    depth = w_in.shape[0]
    assert depth == 1, "final norm is fused into the (single) layer's output projection"
    bp, t, d = x_prompt.shape
    assert bp == 1
    db, tn, _ = x_sample.shape
    n_pool, page, n_heads, hd = cache_k.shape[1:]
    aw = n_heads * hd
    cw = conv_w.shape[2]
    assert aw == cw

    l = 0
    wa_b = w_attn_out[l].astype(_BF16)
    wc_b = w_conv_out[l].astype(_BF16)
    wo_b = w_out[l].astype(_BF16)
    ck = cache_k.reshape(depth * n_pool, page * n_heads, hd)
    cv = cache_v.reshape(depth * n_pool, page * n_heads, hd)

    def attend_prompt(q, k, v, rest):
        return _attn_prompt(q, k, v, rest, rel_bias, n_heads, hd)

    def conv_prompt(rest):
        c, st = _conv_prompt(rest, conv_w[l], cw)
        return c, st[8 - (conv_w.shape[1] - 1):]

    yp, kp, vp, cp, w_in_b = _layer(x_prompt.reshape(t, d), norm_gain[l], w_in[l], conv_w[l], wa_b, wc_b, wo_b,
                                    final_gain, aw, cw, attend_prompt, conv_prompt)

    def attend_sample(q, k, v, rest):
        r3 = lambda z: z.reshape(db, tn, z.shape[-1])
        return _attn_sample(r3(q), r3(k), r3(v), rest, ck, cv, page_table + l * n_pool, rel_bias, n_heads, hd)

    def conv_sample(rest):
        c, u = _conv_sample(rest, state_conv[l], conv_w[l], cw, tn)
        return c, u.reshape(db, tn, cw)[:, tn - (conv_w.shape[1] - 1):]

    ys, ks, vs, cs, _ = _layer(x_sample.reshape(db * tn, d), norm_gain[l], w_in_b, conv_w[l], wa_b, wc_b, wo_b,
                               final_gain, aw, cw, attend_sample, conv_sample)

    return (yp.reshape(1, t, d), ys.reshape(db, tn, d),
            kp.reshape(1, 1, t, n_heads, hd), vp.reshape(1, 1, t, n_heads, hd),
            cp.reshape(1, 1, conv_w.shape[1] - 1, cw),
            ks.reshape(1, db, tn, n_heads, hd), vs.reshape(1, db, tn, n_heads, hd),
            cs.reshape(1, db, conv_w.shape[1] - 1, cw))
```

```python
import functools
import math

import numpy as np
import jax
import jax.numpy as jnp
from jax import lax
from jax.experimental import pallas as pl
from jax.experimental.pallas import tpu as pltpu

MOBA_BLOCK = 256
MOBA_TOPK = 3
N_BUCKETS = 32
MAX_DISTANCE = 128
EPS = 1e-6
MASKED = -1e30
LANES = 128
SUBLANES = 8
SUM_ROWS = 16
NORM_ROWS = 64
LOG2E = math.log2(math.e)
FAR_GROUP = 2
HEADS_PER_STEP = 2
PAGES_PER_STEP = 8
VMEM_LIMIT_BYTES = 56 * 1024 * 1024

_BF16 = jnp.bfloat16
_F32 = jnp.float32


def _params(*semantics):
    return pltpu.CompilerParams(dimension_semantics=semantics, vmem_limit_bytes=VMEM_LIMIT_BYTES)


def _tile(dim, want):
    t = min(dim, want)
    while dim % t:
        t -= LANES
        assert t > 0, (dim, want)
    return t


def _rmsnorm_kernel(x_ref, g_ref, o_ref):
    x = x_ref[...]
    ms = jnp.mean(x * x, axis=-1, keepdims=True)
    o_ref[...] = ((x * lax.rsqrt(ms + EPS)) * g_ref[...]).astype(o_ref.dtype)


def _rmsnorm_bf16(x, gain):
    m, d = x.shape
    tm = _tile(m, 256)
    return pl.pallas_call(
        _rmsnorm_kernel,
        grid=(m // tm,),
        in_specs=[pl.BlockSpec((tm, d), lambda i: (i, 0)), pl.BlockSpec((1, d), lambda i: (0, 0))],
        out_specs=pl.BlockSpec((tm, d), lambda i: (i, 0)),
        out_shape=jax.ShapeDtypeStruct((m, d), _BF16),
        compiler_params=_params("parallel"),
        name="rmsnorm",
    )(x, gain.reshape(1, d))


def _proj_kernel(h_ref, w_ref, o_ref):
    o_ref[...] = jnp.dot(h_ref[...], w_ref[...], preferred_element_type=_F32).astype(o_ref.dtype)


def _proj_cast_kernel(h_ref, w_ref, o_ref, wb_ref):
    @pl.when(pl.program_id(1) == 0)
    def _():
        wb_ref[...] = w_ref[...].astype(wb_ref.dtype)

    o_ref[...] = jnp.dot(h_ref[...], wb_ref[...], preferred_element_type=_F32).astype(o_ref.dtype)


def _proj(h, w, col0, ncols, name, cast=False):
    m, k = h.shape
    tm = _tile(m, 1024)
    tn = _tile(math.gcd(ncols, col0), 512 if cast else 1024)
    c0 = col0 // tn
    in_specs = [pl.BlockSpec((tm, k), lambda j, i: (i, 0)),
                pl.BlockSpec((k, tn), lambda j, i: (0, c0 + j))]
    out_spec = pl.BlockSpec((tm, tn), lambda j, i: (i, j))
    out_shape = jax.ShapeDtypeStruct((m, ncols), _F32)
    if cast:
        out_spec = [out_spec, pl.BlockSpec((k, tn), lambda j, i: (0, j))]
        out_shape = [out_shape, jax.ShapeDtypeStruct((k, ncols), _BF16)]
    return pl.pallas_call(
        _proj_cast_kernel if cast else _proj_kernel,
        grid=(ncols // tn, m // tm),
        in_specs=in_specs,
        out_specs=out_spec,
        out_shape=out_shape,
        compiler_params=_params("parallel", "arbitrary"),
        name=name,
    )(h, w)


def _t5_bucket(dist):
    max_exact = N_BUCKETS // 2
    n = jnp.maximum(dist, 0)
    nf = jnp.maximum(n, 1).astype(_F32)
    large = max_exact + (jnp.log(nf / max_exact) / math.log(MAX_DISTANCE / max_exact)
                         * (N_BUCKETS - max_exact)).astype(jnp.int32)
    large = jnp.minimum(large, N_BUCKETS - 1)
    return jnp.where(n < max_exact, n, large)


def _bias_table(rel_bias, n_dist):
    return rel_bias[_t5_bucket(jnp.arange(n_dist, dtype=jnp.int32))].T.astype(_F32)


def _toeplitz(w):
    h, two_n = w.shape
    n = two_n // 2
    flat = jnp.tile(w, (1, n))[:, :n * (two_n - 1)]
    return flat.reshape(h, n, two_n - 1)[:, :, :n]


def _topk_mask(gate, n_valid, axis):
    idx = lax.broadcasted_iota(jnp.int32, gate.shape, axis).astype(_F32)
    g = jnp.where(idx < n_valid, gate, -jnp.inf)
    sel = jnp.zeros(gate.shape, _F32)
    for _ in range(MOBA_TOPK):
        mx = jnp.max(g, axis=axis, keepdims=True)
        first = jnp.min(jnp.where(g == mx, idx, float(gate.shape[axis])), axis=axis, keepdims=True)
        pick = idx == jnp.where(mx > -jnp.inf, first, -1.0)
        sel = jnp.where(pick, 1.0, sel)
        g = jnp.where(pick, -jnp.inf, g)
    return sel


def _nt_dot(a, b, precision=None):
    return lax.dot_general(a, b, (((1,), (1,)), ((), ())), precision=precision, preferred_element_type=_F32)


def _attn_prompt_kernel(far_ref, q_ref, k_ref, v_ref, g_ref, bown_ref, bprev_ref, o_ref,
                        kb_ref, vt_ref, km_ref, sel_ref, s_ref, *, qscale):
    hp = pl.program_id(0)
    i = pl.program_id(1)
    n_h, nb, blk, hd = kb_ref.shape
    head_cols = [slice(e * hd, (e + 1) * hd) for e in range(n_h)]

    @pl.when(i == 0)
    def _():
        def load_block(c, _):
            rows = pl.ds(pl.multiple_of(c * blk, blk), blk)
            for e in range(n_h):
                kf = k_ref[rows, head_cols[e]]
                kb_ref[e, c] = kf.astype(_BF16)
                vt_ref[e, c, :hd, :] = v_ref[rows, head_cols[e]].T.astype(_BF16)
                extra = lax.broadcasted_iota(jnp.int32, (SUM_ROWS, blk), 0) == 0
                vt_ref[e, c, hd:, :] = jnp.where(extra, 1.0, 0.0).astype(_BF16)
                km_ref[e, pl.ds(c, 1), :] = jnp.mean(kf, axis=0, keepdims=True)
            return 0

        lax.fori_loop(0, nb, load_block, 0)

    def scores(e, j, qtb):
        return jnp.dot(kb_ref[e, j], qtb, preferred_element_type=_F32)

    def weighted_values(e, j, p):
        return jnp.dot(vt_ref[e, j], p.astype(_BF16), preferred_element_type=_F32)

    def far_block(gidx, g):
        return jnp.minimum(gidx * FAR_GROUP + g, nb - 1)

    def far_scores(buf, gidx):
        for e in range(n_h):
            for g in range(FAR_GROUP):
                s_ref[buf, e, g] = scores(e, far_block(gidx, g), qtbs[e])

    def far_bias_row(e, gidx, g):
        visible = gidx * FAR_GROUP + g < i - 1
        return jnp.where(sel_ref[e, pl.ds(far_block(gidx, g), 1), :] > 0.0,
                         jnp.where(visible, far_biases[e], MASKED), MASKED)

    def far_softmax(buf, gidx, carry):
        out = []
        for e, (m, acc) in enumerate(carry):
            pens = [far_bias_row(e, gidx, g) for g in range(FAR_GROUP)]
            m_new = m
            for g in range(FAR_GROUP):
                m_new = jnp.maximum(m_new, jnp.max(s_ref[buf, e, g], axis=0, keepdims=True) + pens[g])
            acc = jnp.exp2(m - m_new) * acc
            for g in range(FAR_GROUP):
                p = jnp.exp2(s_ref[buf, e, g] - (m_new - pens[g]))
                acc = acc + weighted_values(e, far_block(gidx, g), p)
            out.append((m_new, acc))
        return tuple(out)

    def split_bf16(x):
        hi = x.astype(_BF16)
        return hi, (x - hi.astype(_F32)).astype(_BF16)

    jp = jnp.maximum(i - 1, 0)
    far_biases = [far_ref[hp * n_h + e] for e in range(n_h)]
    qts = [q_ref[:, head_cols[e]].T for e in range(n_h)]
    qtbs = [(qt * qscale).astype(_BF16) for qt in qts]
    s_own = [scores(e, i, qtbs[e]) + bown_ref[e] for e in range(n_h)]
    gates = []
    for e in range(n_h):
        q_hi, q_lo = split_bf16(qts[e])
        km_hi, km_lo = split_bf16(km_ref[e])
        gates.append(jnp.dot(km_hi, q_hi, preferred_element_type=_F32)
                     + jnp.dot(km_lo, q_hi, preferred_element_type=_F32)
                     + jnp.dot(km_hi, q_lo, preferred_element_type=_F32))
    far_scores(0, 0)
    s_prev = [scores(e, jp, qtbs[e]) + bprev_ref[e] for e in range(n_h)]
    carry = []
    for e in range(n_h):
        m = jnp.max(s_own[e], axis=0, keepdims=True)
        acc = weighted_values(e, i, jnp.exp2(s_own[e] - m))
        sel_ref[e] = _topk_mask(gates[e], i.astype(_F32), axis=0)
        pen_prev = jnp.where(sel_ref[e, pl.ds(jp, 1), :] > 0.0, 0.0, MASKED)
        m_new = jnp.maximum(m, jnp.max(s_prev[e], axis=0, keepdims=True) + pen_prev)
        acc = jnp.exp2(m - m_new) * acc + weighted_values(e, jp, jnp.exp2(s_prev[e] - (m_new - pen_prev)))
        carry.append((m_new, acc))

    def far_pair(t, carry):
        far_scores(1, 2 * t + 1)
        carry = far_softmax(0, 2 * t, carry)
        far_scores(0, 2 * t + 2)
        return far_softmax(1, 2 * t + 1, carry)

    n_groups = (jnp.maximum(i - 1, 0) + FAR_GROUP - 1) // FAR_GROUP
    far_scores(0, 0)
    carry = lax.fori_loop(0, (n_groups + 1) // 2, far_pair, tuple(carry))
    for e, (m, acc) in enumerate(carry):
        g = g_ref[:, head_cols[e]]
        out = acc[:hd, :] / acc[hd:hd + 1, :]
        o_ref[:, head_cols[e]] = (out.T * (g * jax.nn.sigmoid(g))).astype(o_ref.dtype)


def _attn_prompt(q, k, v, rest, rel_bias, n_heads, hd):
    t = q.shape[0]
    blk = MOBA_BLOCK
    assert t % blk == 0 and hd == LANES
    nb = t // blk
    assert blk + 1 >= MAX_DISTANCE
    tab = _bias_table(rel_bias, 2 * blk)
    bown = _toeplitz(jnp.concatenate([tab[:, :blk], jnp.full((n_heads, blk), MASKED, _F32)], axis=1)) * LOG2E
    bprev = _toeplitz(jnp.concatenate([tab[:, blk:], tab[:, :blk]], axis=1)) * LOG2E
    far = rel_bias[N_BUCKETS - 1].astype(_F32) * LOG2E
    n_h = math.gcd(n_heads, HEADS_PER_STEP)
    kernel = functools.partial(_attn_prompt_kernel, qscale=hd ** -0.5 * LOG2E)
    return pl.pallas_call(
        kernel,
        grid=(n_heads // n_h, nb),
        in_specs=[pl.BlockSpec(memory_space=pltpu.SMEM),
                  pl.BlockSpec((blk, n_h * hd), lambda h, i: (i, h)),
                  pl.BlockSpec((t, n_h * hd), lambda h, i: (0, h)),
                  pl.BlockSpec((t, n_h * hd), lambda h, i: (0, h)),
                  pl.BlockSpec((blk, n_h * hd), lambda h, i: (i, h)),
                  pl.BlockSpec((n_h, blk, blk), lambda h, i: (h, 0, 0)),
                  pl.BlockSpec((n_h, blk, blk), lambda h, i: (h, 0, 0))],
        out_specs=pl.BlockSpec((blk, n_h * hd), lambda h, i: (i, h)),
        out_shape=jax.ShapeDtypeStruct((t, n_heads * hd), _BF16),
        scratch_shapes=[pltpu.VMEM((n_h, nb, blk, hd), _BF16), pltpu.VMEM((n_h, nb, hd + SUM_ROWS, blk), _BF16),
                        pltpu.VMEM((n_h, nb, hd), _F32), pltpu.VMEM((n_h, nb, blk), _F32),
                        pltpu.VMEM((2, n_h, FAR_GROUP, blk, blk), _F32)],
        compiler_params=_params("parallel", "arbitrary"),
        name="attn_prompt",
    )(far, q, k, v, rest, bown, bprev)


def _attn_sample_kernel(pt_ref, q_ref, qb_ref, *refs, n_heads, tn, pages_per_block):
    pps = (len(refs) - 11) // 2
    k_refs, v_refs = refs[:pps], refs[pps:2 * pps]
    (bfar_ref, blast_ref, kn_ref, vn_ref, bnew_ref, g_ref, o_ref, m_ref, l_ref, pv_ref, ks_ref) = refs[2 * pps:]
    del pt_ref
    jj = pl.program_id(1)
    n_pages, rows, hd = pv_ref.shape
    nblk = ks_ref.shape[0] // n_heads
    n_steps = n_pages // pps
    qb = qb_ref[...]

    for s in range(pps):
        page = jj * pps + s
        kf = k_refs[s][...]
        bias = (blast_ref if s == pps - 1 else bfar_ref)[...]
        sc = _nt_dot(qb, kf.astype(_BF16)) + bias
        mp = jnp.max(sc, axis=-1, keepdims=True)
        p = jnp.exp2(sc - mp)
        lp = jnp.sum(p, axis=-1, keepdims=True)
        pv_ref[page] = jnp.dot(p.astype(_BF16), v_refs[s][...].astype(_BF16), preferred_element_type=_F32)
        m_ref[page] = jnp.broadcast_to(mp, (rows, hd))
        l_ref[page] = jnp.broadcast_to(lp, (rows, hd))
        ksum = jnp.sum(kf.reshape(kf.shape[0] // n_heads, n_heads, hd), axis=0)
        blk_rows = pl.ds(pl.multiple_of((page // pages_per_block) * n_heads, n_heads), n_heads)
        if s % pages_per_block == 0:
            ks_ref[blk_rows, :] = ksum
        else:
            ks_ref[blk_rows, :] = ks_ref[blk_rows, :] + ksum

    @pl.when(jj == n_steps - 1)
    def _():
        q32 = q_ref[...]
        g2 = _nt_dot(q32, ks_ref[...], precision=lax.Precision.HIGHEST)
        row_head = lax.broadcasted_iota(jnp.int32, g2.shape, 0) // tn
        col_head = lax.broadcasted_iota(jnp.int32, g2.shape, 1) % n_heads
        own_head = jnp.where(col_head == row_head, g2, 0.0)
        fold = (lax.broadcasted_iota(jnp.int32, (g2.shape[1], LANES), 0) // n_heads
                == lax.broadcasted_iota(jnp.int32, (g2.shape[1], LANES), 1)).astype(_F32)
        gate = jnp.dot(own_head, fold, precision=lax.Precision.HIGHEST, preferred_element_type=_F32)
        sel = _topk_mask(gate, float(nblk), axis=1)

        s_new = _nt_dot(qb, kn_ref[...]) + bnew_ref[...]
        m_new = jnp.max(s_new, axis=-1, keepdims=True)
        live = [jnp.broadcast_to(sel[:, b:b + 1], (rows, hd)) > 0.0 for b in range(nblk)]
        m_tot = jnp.broadcast_to(m_new, (rows, hd))
        for pg in range(n_pages):
            m_tot = jnp.maximum(m_tot, jnp.where(live[pg // pages_per_block], m_ref[pg], MASKED))
        p_new = jnp.exp2(s_new - m_tot[:, :s_new.shape[1]])
        l = jnp.broadcast_to(jnp.sum(p_new, axis=-1, keepdims=True), (rows, hd))
        acc = jnp.dot(p_new.astype(_BF16), vn_ref[...], preferred_element_type=_F32)
        for pg in range(n_pages):
            w = jnp.where(live[pg // pages_per_block], jnp.exp2(m_ref[pg] - m_tot), 0.0)
            l = l + w * l_ref[pg]
            acc = acc + w * pv_ref[pg]
        out = acc / l
        for pos in range(SUBLANES // tn):
            @pl.when(pl.program_id(0) % (SUBLANES // tn) == pos)
            def _(pos=pos):
                mine = slice(pos * tn, (pos + 1) * tn)
                g = g_ref[mine, :]
                gated = g * jax.nn.sigmoid(g)
                for hh in range(n_heads):
                    cols = slice(hh * hd, (hh + 1) * hd)
                    o_ref[mine, cols] = (out[hh * tn:(hh + 1) * tn, :] * gated[:, cols]).astype(o_ref.dtype)


def _attn_sample(q, k_new, v_new, rest, cache_k, cache_v, page_table, rel_bias, n_heads, hd):
    b, tn, width = q.shape
    n_pages = page_table.shape[1]
    page = cache_k.shape[1] // n_heads
    blk = MOBA_BLOCK
    pps = math.gcd(n_pages, PAGES_PER_STEP)
    past = n_pages * page
    assert blk % page == 0 and past % blk == 0 and n_pages <= LANES
    assert hd == LANES and page + 1 >= MAX_DISTANCE
    ppb = blk // page
    assert pps % ppb == 0
    nblk = past // blk
    rows = n_heads * tn
    cols = page * n_heads

    qh = q.reshape(b, tn, n_heads, hd).transpose(0, 2, 1, 3).reshape(b, rows, hd)
    kn = k_new.reshape(b, tn * n_heads, hd).astype(_BF16)
    vn = v_new.reshape(b, tn * n_heads, hd).astype(_BF16)

    head_of_row = np.repeat(np.arange(n_heads), tn)
    t_of_row = np.tile(np.arange(tn), n_heads)
    diag = jnp.asarray(head_of_row[:, None] == np.tile(np.arange(n_heads), page)[None, :])
    tab = _bias_table(rel_bias, page + tn)
    far = rel_bias[N_BUCKETS - 1].astype(_F32)
    bias_far = jnp.where(diag, far[head_of_row][:, None], MASKED)
    d_last = page + np.arange(tn)[:, None] - np.arange(page)[None, :]
    b_last = tab[:, d_last]
    b_last = jnp.broadcast_to(b_last[:, :, :, None], (n_heads, tn, page, n_heads)).reshape(rows, cols)
    bias_pages = jnp.stack([bias_far, jnp.where(diag, b_last, MASKED)]) * LOG2E
    d_new = t_of_row[:, None] - np.repeat(np.arange(tn), n_heads)[None, :]
    same_head = head_of_row[:, None] == np.tile(np.arange(n_heads), tn)[None, :]
    b_new = tab[head_of_row[:, None], np.maximum(d_new, 0)]
    bias_new = jnp.where(jnp.asarray(same_head & (d_new >= 0)), b_new, MASKED) * LOG2E
    qb = (qh * (hd ** -0.5 * LOG2E)).astype(_BF16)

    n_steps = n_pages // pps
    kernel = functools.partial(_attn_sample_kernel, n_heads=n_heads, tn=tn,
                               pages_per_block=ppb)
    page_spec = lambda s: pl.BlockSpec((None, cols, hd), lambda bi, jj, pt: (pt[bi, jj * pps + s], 0, 0))
    per_b = lambda shape: pl.BlockSpec((None,) + shape, lambda bi, jj, pt: (bi, 0, 0))
    assert SUBLANES % tn == 0 and (b * tn) % SUBLANES == 0
    shared_rows = pl.BlockSpec((SUBLANES, width), lambda bi, jj, pt: (bi * tn // SUBLANES, 0))
    grid_spec = pltpu.PrefetchScalarGridSpec(
        num_scalar_prefetch=1,
        grid=(b, n_steps),
        in_specs=[per_b((rows, hd)), per_b((rows, hd))]
                 + [page_spec(s) for s in range(pps)] + [page_spec(s) for s in range(pps)]
                 + [pl.BlockSpec((None, rows, cols), lambda bi, jj, pt: (0, 0, 0)),
                    pl.BlockSpec((None, rows, cols), lambda bi, jj, pt: (jj // (n_steps - 1) if n_steps > 1 else 1, 0, 0)),
                    per_b((tn * n_heads, hd)), per_b((tn * n_heads, hd)),
                    pl.BlockSpec((rows, tn * n_heads), lambda bi, jj, pt: (0, 0)),
                    shared_rows],
        out_specs=shared_rows,
        scratch_shapes=[pltpu.VMEM((n_pages, rows, hd), _F32), pltpu.VMEM((n_pages, rows, hd), _F32),
                        pltpu.VMEM((n_pages, rows, hd), _F32), pltpu.VMEM((nblk * n_heads, hd), _F32)],
    )
    return pl.pallas_call(
        kernel,
        grid_spec=grid_spec,
        out_shape=jax.ShapeDtypeStruct((b * tn, width), _F32),
        compiler_params=_params("arbitrary", "arbitrary"),
        name="attn_sample",
    )(page_table, qh, qb, *([cache_k] * pps), *([cache_v] * pps),
      bias_pages, bias_pages, kn, vn, bias_new, rest).astype(_BF16)


def _conv_combine(u, prev1, prev2, w_ref, b_ref, g_ref, o_ref):
    y = w_ref[0:1, :] * prev2 + w_ref[1:2, :] * prev1 + w_ref[2:3, :] * u
    g = g_ref[...]
    o_ref[...] = (b_ref[...] * y * (g * jax.nn.sigmoid(g))).astype(o_ref.dtype)


def _conv_prompt_kernel(b_ref, c_ref, h_ref, g_ref, ch_ref, hh_ref, w_ref, o_ref, st_ref):
    i = pl.program_id(1)
    u = c_ref[...] * h_ref[...]
    halo = ch_ref[...] * hh_ref[...] * (i > 0).astype(_F32)
    row = lax.broadcasted_iota(jnp.int32, u.shape, 0)
    prev1 = jnp.where(row == 0, halo[7:8, :], pltpu.roll(u, 1, axis=0))
    prev2 = jnp.where(row == 0, halo[6:7, :], jnp.where(row == 1, halo[7:8, :], pltpu.roll(u, 2, axis=0)))
    _conv_combine(u, prev1, prev2, w_ref, b_ref, g_ref, o_ref)
    st_ref[...] = u[u.shape[0] - 8:, :]


def _conv_sample_kernel(b_ref, c_ref, h_ref, g_ref, s1_ref, s2_ref, w_ref, o_ref, u_ref, *, tn):
    u = c_ref[...] * h_ref[...]
    t = lax.broadcasted_iota(jnp.int32, u.shape, 0) % tn
    prev1 = jnp.where(t >= 1, pltpu.roll(u, 1, axis=0), s1_ref[...])
    prev2 = jnp.where(t >= 2, pltpu.roll(u, 2, axis=0), s2_ref[...])
    _conv_combine(u, prev1, prev2, w_ref, b_ref, g_ref, o_ref)
    u_ref[...] = u


def _rest_cols(cw, tc):
    return [(1 + n) * cw // tc for n in range(4)]


def _conv_prompt(rest, conv_w, cw):
    t = rest.shape[0]
    tm = _tile(t, 512)
    tc = _tile(cw, 512)
    ob, oc, oh, og = _rest_cols(cw, tc)
    main = lambda off: pl.BlockSpec((tm, tc), lambda j, i: (i, off + j))
    halo = lambda off: pl.BlockSpec((8, tc), lambda j, i: (jnp.maximum(i * (tm // 8) - 1, 0), off + j))
    return pl.pallas_call(
        _conv_prompt_kernel,
        grid=(cw // tc, t // tm),
        in_specs=[main(ob), main(oc), main(oh), main(og), halo(oc), halo(oh),
                  pl.BlockSpec((conv_w.shape[0], tc), lambda j, i: (0, j))],
        out_specs=[pl.BlockSpec((tm, tc), lambda j, i: (i, j)),
                   pl.BlockSpec((8, tc), lambda j, i: (0, j))],
        out_shape=[jax.ShapeDtypeStruct((t, cw), _BF16), jax.ShapeDtypeStruct((8, cw), _F32)],
        compiler_params=_params("parallel", "arbitrary"),
        name="conv_prompt",
    )(rest, rest, rest, rest, rest, rest, conv_w)


def _conv_sample(rest, state, conv_w, cw, tn):
    m = rest.shape[0]
    b = m // tn
    assert tn >= 2 and conv_w.shape[0] == 3
    tc = _tile(cw, 512)
    ob, oc, oh, og = _rest_cols(cw, tc)
    zeros = jnp.zeros((b, tn - 1, cw), _F32)
    s1 = jnp.concatenate([state[:, 1:2], zeros], axis=1).reshape(m, cw)
    s2 = jnp.concatenate([state, zeros[:, 1:]], axis=1).reshape(m, cw)
    main = lambda off: pl.BlockSpec((m, tc), lambda j: (0, off + j))
    own = pl.BlockSpec((m, tc), lambda j: (0, j))
    return pl.pallas_call(
        functools.partial(_conv_sample_kernel, tn=tn),
        grid=(cw // tc,),
        in_specs=[main(ob), main(oc), main(oh), main(og), own, own,
                  pl.BlockSpec((conv_w.shape[0], tc), lambda j: (0, j))],
        out_specs=[own, own],
        out_shape=[jax.ShapeDtypeStruct((m, cw), _BF16), jax.ShapeDtypeStruct((m, cw), _F32)],
        compiler_params=_params("parallel"),
        name="conv_sample",
    )(rest, rest, rest, rest, s1, s2, conv_w)


def _merge_kernel(a_ref, c_ref, wa_ref, wc_ref, ma_ref, mc_ref, o_ref):
    ya = jnp.dot(a_ref[...], wa_ref[...], preferred_element_type=_F32)
    yc = jnp.dot(c_ref[...], wc_ref[...], preferred_element_type=_F32)
    o_ref[...] = (jax.nn.sigmoid(ma_ref[...]) * ya + jax.nn.sigmoid(mc_ref[...]) * yc).astype(o_ref.dtype)


def _merge_cast_kernel(a_ref, c_ref, wa_ref, wc_ref, ma_ref, mc_ref, o_ref, wab_ref, wcb_ref):
    @pl.when(pl.program_id(1) == 0)
    def _():
        wab_ref[...] = wa_ref[...].astype(wab_ref.dtype)
        wcb_ref[...] = wc_ref[...].astype(wcb_ref.dtype)

    _merge_kernel(a_ref, c_ref, wab_ref, wcb_ref, ma_ref, mc_ref, o_ref)


def _merge(a, c, wa, wc, rest, gate_col0, cast=False):
    m, ka = a.shape
    kc = c.shape[1]
    d = wa.shape[1]
    tm = _tile(m, 512)
    tn = _tile(math.gcd(d, gate_col0), 512 if cast else 1024)
    oa = gate_col0 // tn
    oc = (gate_col0 + d) // tn
    out_spec = pl.BlockSpec((tm, tn), lambda j, i: (i, j))
    out_shape = jax.ShapeDtypeStruct((m, d), _BF16)
    if cast:
        out_spec = [out_spec, pl.BlockSpec((ka, tn), lambda j, i: (0, j)), pl.BlockSpec((kc, tn), lambda j, i: (0, j))]
        out_shape = [out_shape, jax.ShapeDtypeStruct((ka, d), _BF16), jax.ShapeDtypeStruct((kc, d), _BF16)]
    return pl.pallas_call(
        _merge_cast_kernel if cast else _merge_kernel,
        grid=(d // tn, m // tm),
        in_specs=[pl.BlockSpec((tm, ka), lambda j, i: (i, 0)),
                  pl.BlockSpec((tm, kc), lambda j, i: (i, 0)),
                  pl.BlockSpec((ka, tn), lambda j, i: (0, j)),
                  pl.BlockSpec((kc, tn), lambda j, i: (0, j)),
                  pl.BlockSpec((tm, tn), lambda j, i: (i, oa + j)),
                  pl.BlockSpec((tm, tn), lambda j, i: (i, oc + j))],
        out_specs=out_spec,
        out_shape=out_shape,
        compiler_params=_params("parallel", "arbitrary"),
        name="merge",
    )(a, c, wa, wc, rest, rest)


def _out_kernel(mg_ref, w_ref, x_ref, g_ref, o_ref, *maybe_wb_ref, tn, nj):
    j = pl.program_id(1)
    if maybe_wb_ref:
        maybe_wb_ref[0][...] = w_ref[...].astype(maybe_wb_ref[0].dtype)
        w_ref = maybe_wb_ref[0]
    z = x_ref[...] + jnp.dot(mg_ref[...], w_ref[...], preferred_element_type=_F32)
    for jj in range(nj):
        @pl.when(j == jj)
        def _(jj=jj):
            o_ref[:, jj * tn:(jj + 1) * tn] = z

    @pl.when(j == nj - 1)
    def _():
        nr = math.gcd(o_ref.shape[0], NORM_ROWS)

        def norm_rows(r, _):
            rows = pl.ds(pl.multiple_of(r * nr, nr), nr)
            y = o_ref[rows, :]
            ms = jnp.mean(y * y, axis=-1, keepdims=True)
            o_ref[rows, :] = (y * lax.rsqrt(ms + EPS)) * g_ref[...]
            return 0

        lax.fori_loop(0, o_ref.shape[0] // nr, norm_rows, 0)


def _out(merged, w, x, gain, cast=False):
    m, d = x.shape
    tm = _tile(m, 512)
    tn = _tile(d, 256 if cast else 1024)
    nj = d // tn
    out_spec = pl.BlockSpec((tm, d), lambda i, j: (i, 0))
    out_shape = jax.ShapeDtypeStruct((m, d), _F32)
    if cast:
        assert m == tm
        out_spec = [out_spec, pl.BlockSpec((d, tn), lambda i, j: (0, j))]
        out_shape = [out_shape, jax.ShapeDtypeStruct((d, d), _BF16)]
    return pl.pallas_call(
        functools.partial(_out_kernel, tn=tn, nj=nj),
        grid=(m // tm, nj),
        in_specs=[pl.BlockSpec((tm, d), lambda i, j: (i, 0)),
                  pl.BlockSpec((d, tn), lambda i, j: (0, j)),
                  pl.BlockSpec((tm, tn), lambda i, j: (i, j)),
                  pl.BlockSpec((1, d), lambda i, j: (0, 0))],
        out_specs=out_spec,
        out_shape=out_shape,
        compiler_params=_params("parallel", "arbitrary"),
        name="out_proj",
    )(merged, w, x, gain.reshape(1, d))


def _layer(x, gain, weights, out_gain, aw, cw, attend, conv):
    w_in, wa, wc, w_out = weights
    cast = not isinstance(w_in, (list, tuple))
    h = _rmsnorm_bf16(x, gain)
    names = ("proj_q", "proj_k", "proj_v", "proj_rest")
    if cast:
        col0s = (0, aw, 2 * aw, 3 * aw)
        widths = (aw, aw, aw, w_in.shape[1] - 3 * aw)
        outs = [_proj(h, w_in, c0, n, name, cast=True) for c0, n, name in zip(col0s, widths, names)]
        (q, k, v, rest), w_in = zip(*outs)
    else:
        q, k, v, rest = (_proj(h, w, 0, w.shape[1], name) for w, name in zip(w_in, names))
    a = attend(q, k, v, rest)
    c, state = conv(rest)
    if cast:
        merged, wa, wc = _merge(a, c, wa, wc, rest, aw + 4 * cw, cast=True)
        y, w_out = _out(merged, w_out, x, out_gain, cast=True)
    else:
        y = _out(_merge(a, c, wa, wc, rest, aw + 4 * cw), w_out, x, out_gain)
    return y, k, v, state, (list(w_in), wa, wc, w_out)


def kernel(x_prompt, x_sample, cache_k, cache_v, state_conv, page_table, norm_gain, w_in,
           conv_w, w_attn_out, w_conv_out, w_out, rel_bias, final_gain):
    depth = w_in.shape[0]
    assert depth == 1, "final norm is fused into the (single) layer's output projection"
    bp, t, d = x_prompt.shape
    assert bp == 1
    db, tn, _ = x_sample.shape
    n_pool, page, n_heads, hd = cache_k.shape[1:]
    aw = n_heads * hd
    cw = conv_w.shape[2]
    assert aw == cw

    l = 0
    ck = cache_k.reshape(depth * n_pool, page * n_heads, hd)
    cv = cache_v.reshape(depth * n_pool, page * n_heads, hd)

    def attend_prompt(q, k, v, rest):
        return _attn_prompt(q, k, v, rest, rel_bias, n_heads, hd)

    def conv_prompt(rest):
        c, st = _conv_prompt(rest, conv_w[l], cw)
        return c, st[8 - (conv_w.shape[1] - 1):]

    def attend_sample(q, k, v, rest):
        r3 = lambda z: z.reshape(db, tn, z.shape[-1])
        return _attn_sample(r3(q), r3(k), r3(v), rest, ck, cv, page_table + l * n_pool, rel_bias, n_heads, hd)

    def conv_sample(rest):
        c, u = _conv_sample(rest, state_conv[l], conv_w[l], cw, tn)
        return c, u.reshape(db, tn, cw)[:, tn - (conv_w.shape[1] - 1):]

    weights = (w_in[l], w_attn_out[l], w_conv_out[l], w_out[l])
    ys, ks, vs, cs, weights_b = _layer(x_sample.reshape(db * tn, d), norm_gain[l], weights,
                                       final_gain, aw, cw, attend_sample, conv_sample)
    yp, kp, vp, cp, _ = _layer(x_prompt.reshape(t, d), norm_gain[l], weights_b,
                               final_gain, aw, cw, attend_prompt, conv_prompt)

    return (yp.reshape(1, t, d), ys.reshape(db, tn, d),
            kp.reshape(1, 1, t, n_heads, hd), vp.reshape(1, 1, t, n_heads, hd),
            cp.reshape(1, 1, conv_w.shape[1] - 1, cw),
            ks.reshape(1, db, tn, n_heads, hd), vs.reshape(1, db, tn, n_heads, hd),
            cs.reshape(1, db, conv_w.shape[1] - 1, cw))
```

```python
import functools
import math

import numpy as np
import jax
import jax.numpy as jnp
from jax import lax
from jax.experimental import pallas as pl
from jax.experimental.pallas import tpu as pltpu

MOBA_BLOCK = 256
MOBA_TOPK = 3
N_BUCKETS = 32
MAX_DISTANCE = 128
EPS = 1e-6
MASKED = -1e30
LANES = 128
SUBLANES = 8
SUM_ROWS = 16
NORM_ROWS = 64
LOG2E = math.log2(math.e)
FAR_GROUP = 2
HEADS_PER_STEP = 2
PAGES_PER_STEP = 4
VMEM_LIMIT_BYTES = 56 * 1024 * 1024

_BF16 = jnp.bfloat16
_F32 = jnp.float32


def _params(*semantics):
    return pltpu.CompilerParams(dimension_semantics=semantics, vmem_limit_bytes=VMEM_LIMIT_BYTES)


def _tile(dim, want):
    t = min(dim, want)
    while dim % t:
        t -= LANES
        assert t > 0, (dim, want)
    return t


def _rmsnorm_kernel(x_ref, g_ref, o_ref):
    x = x_ref[...]
    ms = jnp.mean(x * x, axis=-1, keepdims=True)
    o_ref[...] = ((x * lax.rsqrt(ms + EPS)) * g_ref[...]).astype(o_ref.dtype)


def _rmsnorm_bf16(x, gain):
    m, d = x.shape
    tm = _tile(m, 256)
    return pl.pallas_call(
        _rmsnorm_kernel,
        grid=(m // tm,),
        in_specs=[pl.BlockSpec((tm, d), lambda i: (i, 0)), pl.BlockSpec((1, d), lambda i: (0, 0))],
        out_specs=pl.BlockSpec((tm, d), lambda i: (i, 0)),
        out_shape=jax.ShapeDtypeStruct((m, d), _BF16),
        compiler_params=_params("parallel"),
        name="rmsnorm",
    )(x, gain.reshape(1, d))


def _proj_kernel(h_ref, w_ref, o_ref):
    o_ref[...] = jnp.dot(h_ref[...], w_ref[...], preferred_element_type=_F32).astype(o_ref.dtype)


def _proj_cast_kernel(h_ref, w_ref, o_ref, wb_ref):
    @pl.when(pl.program_id(1) == 0)
    def _():
        wb_ref[...] = w_ref[...].astype(wb_ref.dtype)

    o_ref[...] = jnp.dot(h_ref[...], wb_ref[...], preferred_element_type=_F32).astype(o_ref.dtype)


def _proj(h, w, col0, ncols, name, cast=False):
    m, k = h.shape
    tm = _tile(m, 1024)
    tn = _tile(math.gcd(ncols, col0), 512 if cast else 1024)
    c0 = col0 // tn
    in_specs = [pl.BlockSpec((tm, k), lambda j, i: (i, 0)),
                pl.BlockSpec((k, tn), lambda j, i: (0, c0 + j))]
    out_spec = pl.BlockSpec((tm, tn), lambda j, i: (i, j))
    out_shape = jax.ShapeDtypeStruct((m, ncols), _F32)
    if cast:
        out_spec = [out_spec, pl.BlockSpec((k, tn), lambda j, i: (0, j))]
        out_shape = [out_shape, jax.ShapeDtypeStruct((k, ncols), _BF16)]
    return pl.pallas_call(
        _proj_cast_kernel if cast else _proj_kernel,
        grid=(ncols // tn, m // tm),
        in_specs=in_specs,
        out_specs=out_spec,
        out_shape=out_shape,
        compiler_params=_params("parallel", "arbitrary"),
        name=name,
    )(h, w)


def _t5_bucket(dist):
    max_exact = N_BUCKETS // 2
    n = jnp.maximum(dist, 0)
    nf = jnp.maximum(n, 1).astype(_F32)
    large = max_exact + (jnp.log(nf / max_exact) / math.log(MAX_DISTANCE / max_exact)
                         * (N_BUCKETS - max_exact)).astype(jnp.int32)
    large = jnp.minimum(large, N_BUCKETS - 1)
    return jnp.where(n < max_exact, n, large)


def _bias_table(rel_bias, n_dist):
    return rel_bias[_t5_bucket(jnp.arange(n_dist, dtype=jnp.int32))].T.astype(_F32)


def _toeplitz(w):
    h, two_n = w.shape
    n = two_n // 2
    flat = jnp.tile(w, (1, n))[:, :n * (two_n - 1)]
    return flat.reshape(h, n, two_n - 1)[:, :, :n]


def _topk_mask(gate, n_valid, axis):
    idx = lax.broadcasted_iota(jnp.int32, gate.shape, axis).astype(_F32)
    g = jnp.where(idx < n_valid, gate, -jnp.inf)
    sel = jnp.zeros(gate.shape, _F32)
    for _ in range(MOBA_TOPK):
        mx = jnp.max(g, axis=axis, keepdims=True)
        first = jnp.min(jnp.where(g == mx, idx, float(gate.shape[axis])), axis=axis, keepdims=True)
        pick = idx == jnp.where(mx > -jnp.inf, first, -1.0)
        sel = jnp.where(pick, 1.0, sel)
        g = jnp.where(pick, -jnp.inf, g)
    return sel


def _nt_dot(a, b, precision=None):
    return lax.dot_general(a, b, (((1,), (1,)), ((), ())), precision=precision, preferred_element_type=_F32)


def _attn_prompt_kernel(far_ref, q_ref, k_ref, v_ref, g_ref, bown_ref, bprev_ref, o_ref,
                        kb_ref, vt_ref, km_ref, sel_ref, s_ref, *, qscale):
    hp = pl.program_id(0)
    i = pl.program_id(1)
    n_h, nb, blk, hd = kb_ref.shape
    head_cols = [slice(e * hd, (e + 1) * hd) for e in range(n_h)]

    @pl.when(i == 0)
    def _():
        def load_block(c, _):
            rows = pl.ds(pl.multiple_of(c * blk, blk), blk)
            for e in range(n_h):
                kf = k_ref[rows, head_cols[e]]
                kb_ref[e, c] = kf.astype(_BF16)
                vt_ref[e, c, :hd, :] = v_ref[rows, head_cols[e]].T.astype(_BF16)
                extra = lax.broadcasted_iota(jnp.int32, (SUM_ROWS, blk), 0) == 0
                vt_ref[e, c, hd:, :] = jnp.where(extra, 1.0, 0.0).astype(_BF16)
                km_ref[e, pl.ds(c, 1), :] = jnp.mean(kf, axis=0, keepdims=True)
            return 0

        lax.fori_loop(0, nb, load_block, 0)

    def scores(e, j, qtb):
        return jnp.dot(kb_ref[e, j], qtb, preferred_element_type=_F32)

    def weighted_values(e, j, p):
        return jnp.dot(vt_ref[e, j], p.astype(_BF16), preferred_element_type=_F32)

    def far_block(gidx, g):
        return jnp.minimum(gidx * FAR_GROUP + g, nb - 1)

    def far_scores(buf, gidx):
        for e in range(n_h):
            for g in range(FAR_GROUP):
                s_ref[buf, e, g] = scores(e, far_block(gidx, g), qtbs[e])

    def far_bias_row(e, gidx, g):
        visible = gidx * FAR_GROUP + g < i - 1
        return jnp.where(sel_ref[e, pl.ds(far_block(gidx, g), 1), :] > 0.0,
                         jnp.where(visible, far_biases[e], MASKED), MASKED)

    def far_softmax(buf, gidx, carry):
        out = []
        for e, (m, acc) in enumerate(carry):
            pens = [far_bias_row(e, gidx, g) for g in range(FAR_GROUP)]
            m_new = m
            for g in range(FAR_GROUP):
                m_new = jnp.maximum(m_new, jnp.max(s_ref[buf, e, g], axis=0, keepdims=True) + pens[g])
            acc = jnp.exp2(m - m_new) * acc
            for g in range(FAR_GROUP):
                p = jnp.exp2(s_ref[buf, e, g] - (m_new - pens[g]))
                acc = acc + weighted_values(e, far_block(gidx, g), p)
            out.append((m_new, acc))
        return tuple(out)

    def split_bf16(x):
        hi = x.astype(_BF16)
        return hi, (x - hi.astype(_F32)).astype(_BF16)

    jp = jnp.maximum(i - 1, 0)
    far_biases = [far_ref[hp * n_h + e] for e in range(n_h)]
    qts = [q_ref[:, head_cols[e]].T for e in range(n_h)]
    qtbs = [(qt * qscale).astype(_BF16) for qt in qts]
    s_own = [scores(e, i, qtbs[e]) + bown_ref[e] for e in range(n_h)]
    gates = []
    for e in range(n_h):
        q_hi, q_lo = split_bf16(qts[e])
        km_hi, km_lo = split_bf16(km_ref[e])
        gates.append(jnp.dot(km_hi, q_hi, preferred_element_type=_F32)
                     + jnp.dot(km_lo, q_hi, preferred_element_type=_F32)
                     + jnp.dot(km_hi, q_lo, preferred_element_type=_F32))
    far_scores(0, 0)
    s_prev = [scores(e, jp, qtbs[e]) + bprev_ref[e] for e in range(n_h)]
    carry = []
    for e in range(n_h):
        m = jnp.max(s_own[e], axis=0, keepdims=True)
        acc = weighted_values(e, i, jnp.exp2(s_own[e] - m))
        sel_ref[e] = _topk_mask(gates[e], i.astype(_F32), axis=0)
        pen_prev = jnp.where(sel_ref[e, pl.ds(jp, 1), :] > 0.0, 0.0, MASKED)
        m_new = jnp.maximum(m, jnp.max(s_prev[e], axis=0, keepdims=True) + pen_prev)
        acc = jnp.exp2(m - m_new) * acc + weighted_values(e, jp, jnp.exp2(s_prev[e] - (m_new - pen_prev)))
        carry.append((m_new, acc))

    def far_pair(t, carry):
        far_scores(1, 2 * t + 1)
        carry = far_softmax(0, 2 * t, carry)
        far_scores(0, 2 * t + 2)
        return far_softmax(1, 2 * t + 1, carry)

    n_groups = (jnp.maximum(i - 1, 0) + FAR_GROUP - 1) // FAR_GROUP
    far_scores(0, 0)
    carry = lax.fori_loop(0, (n_groups + 1) // 2, far_pair, tuple(carry))
    for e, (m, acc) in enumerate(carry):
        g = g_ref[:, head_cols[e]]
        out = acc[:hd, :] / acc[hd:hd + 1, :]
        o_ref[:, head_cols[e]] = (out.T * (g * jax.nn.sigmoid(g))).astype(o_ref.dtype)


def _attn_prompt(q, k, v, rest, rel_bias, n_heads, hd):
    t = q.shape[0]
    blk = MOBA_BLOCK
    assert t % blk == 0 and hd == LANES
    nb = t // blk
    assert blk + 1 >= MAX_DISTANCE
    tab = _bias_table(rel_bias, 2 * blk)
    bown = _toeplitz(jnp.concatenate([tab[:, :blk], jnp.full((n_heads, blk), MASKED, _F32)], axis=1)) * LOG2E
    bprev = _toeplitz(jnp.concatenate([tab[:, blk:], tab[:, :blk]], axis=1)) * LOG2E
    far = rel_bias[N_BUCKETS - 1].astype(_F32) * LOG2E
    n_h = math.gcd(n_heads, HEADS_PER_STEP)
    kernel = functools.partial(_attn_prompt_kernel, qscale=hd ** -0.5 * LOG2E)
    return pl.pallas_call(
        kernel,
        grid=(n_heads // n_h, nb),
        in_specs=[pl.BlockSpec(memory_space=pltpu.SMEM),
                  pl.BlockSpec((blk, n_h * hd), lambda h, i: (i, h)),
                  pl.BlockSpec((t, n_h * hd), lambda h, i: (0, h)),
                  pl.BlockSpec((t, n_h * hd), lambda h, i: (0, h)),
                  pl.BlockSpec((blk, n_h * hd), lambda h, i: (i, h)),
                  pl.BlockSpec((n_h, blk, blk), lambda h, i: (h, 0, 0)),
                  pl.BlockSpec((n_h, blk, blk), lambda h, i: (h, 0, 0))],
        out_specs=pl.BlockSpec((blk, n_h * hd), lambda h, i: (i, h)),
        out_shape=jax.ShapeDtypeStruct((t, n_heads * hd), _BF16),
        scratch_shapes=[pltpu.VMEM((n_h, nb, blk, hd), _BF16), pltpu.VMEM((n_h, nb, hd + SUM_ROWS, blk), _BF16),
                        pltpu.VMEM((n_h, nb, hd), _F32), pltpu.VMEM((n_h, nb, blk), _F32),
                        pltpu.VMEM((2, n_h, FAR_GROUP, blk, blk), _F32)],
        compiler_params=_params("parallel", "arbitrary"),
        name="attn_prompt",
    )(far, q, k, v, rest, bown, bprev)


def _sample_unit(j, i, n_i, n_steps, n_batch):
    step = j * n_i + i
    b = step // n_steps
    return jnp.minimum(b, n_batch - 1), jnp.where(b < n_batch, step % n_steps, n_steps - 1)


def _attn_sample_kernel(pt_ref, h_ref, w_ref, q_ref, qb_ref, *refs, n_heads, tn, pages_per_block, n_batch):
    pps = (len(refs) - 12) // 2
    k_refs, v_refs = refs[:pps], refs[pps:2 * pps]
    (bfar_ref, blast_ref, kn_ref, vn_ref, bnew_ref, g_ref, o_ref, orest_ref,
     m_ref, l_ref, pv_ref, ks_ref) = refs[2 * pps:]
    del pt_ref
    n_pages, rows, hd = pv_ref.shape
    nblk = ks_ref.shape[0] // n_heads
    n_steps = n_pages // pps
    bi, jj = _sample_unit(pl.program_id(0), pl.program_id(1), pl.num_programs(1), n_steps, n_batch)
    qb = qb_ref[...]
    mm_cols = w_ref.shape[1] // pps
    assert mm_cols % LANES == 0

    for s in range(pps):
        page = jj * pps + s
        kf = k_refs[s][...]
        bias = (blast_ref if s == pps - 1 else bfar_ref)[...]
        sc = _nt_dot(qb, kf.astype(_BF16)) + bias
        chunk = slice(s * mm_cols, (s + 1) * mm_cols)
        orest_ref[:, chunk] = jnp.dot(h_ref[...], w_ref[:, chunk], preferred_element_type=_F32)
        mp = jnp.max(sc, axis=-1, keepdims=True)
        p = jnp.exp2(sc - mp)
        lp = jnp.sum(p, axis=-1, keepdims=True)
        pv_ref[page] = jnp.dot(p.astype(_BF16), v_refs[s][...].astype(_BF16), preferred_element_type=_F32)
        m_ref[page] = jnp.broadcast_to(mp, (rows, hd))
        l_ref[page] = jnp.broadcast_to(lp, (rows, hd))
        ksum = jnp.sum(kf.reshape(kf.shape[0] // n_heads, n_heads, hd), axis=0)
        blk_rows = pl.ds(pl.multiple_of((page // pages_per_block) * n_heads, n_heads), n_heads)
        if s % pages_per_block == 0:
            ks_ref[blk_rows, :] = ksum
        else:
            ks_ref[blk_rows, :] = ks_ref[blk_rows, :] + ksum

    @pl.when(jj == n_steps - 1)
    def _():
        q32 = q_ref[...]
        g2 = _nt_dot(q32, ks_ref[...], precision=lax.Precision.HIGHEST)
        row_head = lax.broadcasted_iota(jnp.int32, g2.shape, 0) // tn
        col_head = lax.broadcasted_iota(jnp.int32, g2.shape, 1) % n_heads
        own_head = jnp.where(col_head == row_head, g2, 0.0)
        fold = (lax.broadcasted_iota(jnp.int32, (g2.shape[1], LANES), 0) // n_heads
                == lax.broadcasted_iota(jnp.int32, (g2.shape[1], LANES), 1)).astype(_F32)
        gate = jnp.dot(own_head, fold, precision=lax.Precision.HIGHEST, preferred_element_type=_F32)
        sel = _topk_mask(gate, float(nblk), axis=1)

        s_new = _nt_dot(qb, kn_ref[...]) + bnew_ref[...]
        m_new = jnp.max(s_new, axis=-1, keepdims=True)
        live = [jnp.broadcast_to(sel[:, b:b + 1], (rows, hd)) > 0.0 for b in range(nblk)]
        m_tot = jnp.broadcast_to(m_new, (rows, hd))
        for pg in range(n_pages):
            m_tot = jnp.maximum(m_tot, jnp.where(live[pg // pages_per_block], m_ref[pg], MASKED))
        p_new = jnp.exp2(s_new - m_tot[:, :s_new.shape[1]])
        l = jnp.broadcast_to(jnp.sum(p_new, axis=-1, keepdims=True), (rows, hd))
        acc = jnp.dot(p_new.astype(_BF16), vn_ref[...], preferred_element_type=_F32)
        for pg in range(n_pages):
            w = jnp.where(live[pg // pages_per_block], jnp.exp2(m_ref[pg] - m_tot), 0.0)
            l = l + w * l_ref[pg]
            acc = acc + w * pv_ref[pg]
        out = acc / l
        for pos in range(SUBLANES // tn):
            @pl.when(bi % (SUBLANES // tn) == pos)
            def _(pos=pos):
                mine = slice(pos * tn, (pos + 1) * tn)
                g = g_ref[mine, :]
                gated = g * jax.nn.sigmoid(g)
                for hh in range(n_heads):
                    cols = slice(hh * hd, (hh + 1) * hd)
                    o_ref[mine, cols] = (out[hh * tn:(hh + 1) * tn, :] * gated[:, cols]).astype(o_ref.dtype)


def _attn_sample(q, k_new, v_new, rest, cache_k, cache_v, page_table, rel_bias, n_heads, hd, h_other, w_other):
    b, tn, width = q.shape
    n_pages = page_table.shape[1]
    page = cache_k.shape[1] // n_heads
    blk = MOBA_BLOCK
    pps = math.gcd(n_pages, PAGES_PER_STEP)
    past = n_pages * page
    assert blk % page == 0 and past % blk == 0 and n_pages <= LANES
    assert hd == LANES and page + 1 >= MAX_DISTANCE
    ppb = blk // page
    assert pps % ppb == 0
    nblk = past // blk
    rows = n_heads * tn
    cols = page * n_heads

    qh = q.reshape(b, tn, n_heads, hd).transpose(0, 2, 1, 3).reshape(b, rows, hd)
    kn = k_new.reshape(b, tn * n_heads, hd).astype(_BF16)
    vn = v_new.reshape(b, tn * n_heads, hd).astype(_BF16)

    head_of_row = np.repeat(np.arange(n_heads), tn)
    t_of_row = np.tile(np.arange(tn), n_heads)
    diag = jnp.asarray(head_of_row[:, None] == np.tile(np.arange(n_heads), page)[None, :])
    tab = _bias_table(rel_bias, page + tn)
    far = rel_bias[N_BUCKETS - 1].astype(_F32)
    bias_far = jnp.where(diag, far[head_of_row][:, None], MASKED)
    d_last = page + np.arange(tn)[:, None] - np.arange(page)[None, :]
    b_last = tab[:, d_last]
    b_last = jnp.broadcast_to(b_last[:, :, :, None], (n_heads, tn, page, n_heads)).reshape(rows, cols)
    bias_pages = jnp.stack([bias_far, jnp.where(diag, b_last, MASKED)]) * LOG2E
    d_new = t_of_row[:, None] - np.repeat(np.arange(tn), n_heads)[None, :]
    same_head = head_of_row[:, None] == np.tile(np.arange(n_heads), tn)[None, :]
    b_new = tab[head_of_row[:, None], np.maximum(d_new, 0)]
    bias_new = jnp.where(jnp.asarray(same_head & (d_new >= 0)), b_new, MASKED) * LOG2E
    qb = (qh * (hd ** -0.5 * LOG2E)).astype(_BF16)

    n_steps = n_pages // pps
    m_o, k_o = h_other.shape
    n_o = w_other.shape[1]
    tm = _tile(m_o, 256)
    tno = max(c for c in range(pps * LANES, 1024 + 1, pps * LANES) if n_o % c == 0)
    n_j, n_i = n_o // tno, m_o // tm
    assert n_j * n_i >= b * n_steps
    kernel = functools.partial(_attn_sample_kernel, n_heads=n_heads, tn=tn, pages_per_block=ppb, n_batch=b)
    unit = lambda j, i: _sample_unit(j, i, n_i, n_steps, b)
    page_spec = lambda s: pl.BlockSpec(
        (None, cols, hd), lambda j, i, pt: (pt[unit(j, i)[0], unit(j, i)[1] * pps + s], 0, 0))
    per_b = lambda shape: pl.BlockSpec((None,) + shape, lambda j, i, pt: (unit(j, i)[0], 0, 0))
    assert SUBLANES % tn == 0 and (b * tn) % SUBLANES == 0
    shared_rows = pl.BlockSpec((SUBLANES, width), lambda j, i, pt: (unit(j, i)[0] * tn // SUBLANES, 0))
    grid_spec = pltpu.PrefetchScalarGridSpec(
        num_scalar_prefetch=1,
        grid=(n_j, n_i),
        in_specs=[pl.BlockSpec((tm, k_o), lambda j, i, pt: (i, 0)),
                  pl.BlockSpec((k_o, tno), lambda j, i, pt: (0, j)),
                  per_b((rows, hd)), per_b((rows, hd))]
                 + [page_spec(s) for s in range(pps)] + [page_spec(s) for s in range(pps)]
                 + [pl.BlockSpec((None, rows, cols), lambda j, i, pt: (0, 0, 0)),
                    pl.BlockSpec((None, rows, cols), lambda j, i, pt: ((unit(j, i)[1] + 1) // n_steps, 0, 0)),
                    per_b((tn * n_heads, hd)), per_b((tn * n_heads, hd)),
                    pl.BlockSpec((rows, tn * n_heads), lambda j, i, pt: (0, 0)),
                    shared_rows],
        out_specs=[shared_rows, pl.BlockSpec((tm, tno), lambda j, i, pt: (i, j))],
        scratch_shapes=[pltpu.VMEM((n_pages, rows, hd), _F32), pltpu.VMEM((n_pages, rows, hd), _F32),
                        pltpu.VMEM((n_pages, rows, hd), _F32), pltpu.VMEM((nblk * n_heads, hd), _F32)],
    )
    attn, other = pl.pallas_call(
        kernel,
        grid_spec=grid_spec,
        out_shape=[jax.ShapeDtypeStruct((b * tn, width), _F32), jax.ShapeDtypeStruct((m_o, n_o), _F32)],
        compiler_params=_params("arbitrary", "arbitrary"),
        name="attn_sample_and_proj",
    )(page_table, h_other, w_other, qh, qb, *([cache_k] * pps), *([cache_v] * pps),
      bias_pages, bias_pages, kn, vn, bias_new, rest)
    return attn.astype(_BF16), other


def _conv_combine(u, prev1, prev2, w_ref, b_ref, g_ref, o_ref):
    y = w_ref[0:1, :] * prev2 + w_ref[1:2, :] * prev1 + w_ref[2:3, :] * u
    g = g_ref[...]
    o_ref[...] = (b_ref[...] * y * (g * jax.nn.sigmoid(g))).astype(o_ref.dtype)


def _conv_prompt_kernel(b_ref, c_ref, h_ref, g_ref, ch_ref, hh_ref, w_ref, o_ref, st_ref):
    i = pl.program_id(1)
    u = c_ref[...] * h_ref[...]
    halo = ch_ref[...] * hh_ref[...] * (i > 0).astype(_F32)
    row = lax.broadcasted_iota(jnp.int32, u.shape, 0)
    prev1 = jnp.where(row == 0, halo[7:8, :], pltpu.roll(u, 1, axis=0))
    prev2 = jnp.where(row == 0, halo[6:7, :], jnp.where(row == 1, halo[7:8, :], pltpu.roll(u, 2, axis=0)))
    _conv_combine(u, prev1, prev2, w_ref, b_ref, g_ref, o_ref)
    st_ref[...] = u[u.shape[0] - 8:, :]


def _conv_sample_kernel(b_ref, c_ref, h_ref, g_ref, s1_ref, s2_ref, w_ref, o_ref, u_ref, *, tn):
    u = c_ref[...] * h_ref[...]
    t = lax.broadcasted_iota(jnp.int32, u.shape, 0) % tn
    prev1 = jnp.where(t >= 1, pltpu.roll(u, 1, axis=0), s1_ref[...])
    prev2 = jnp.where(t >= 2, pltpu.roll(u, 2, axis=0), s2_ref[...])
    _conv_combine(u, prev1, prev2, w_ref, b_ref, g_ref, o_ref)
    u_ref[...] = u


def _rest_cols(cw, tc):
    return [(1 + n) * cw // tc for n in range(4)]


def _conv_prompt(rest, conv_w, cw):
    t = rest.shape[0]
    tm = _tile(t, 512)
    tc = _tile(cw, 512)
    ob, oc, oh, og = _rest_cols(cw, tc)
    main = lambda off: pl.BlockSpec((tm, tc), lambda j, i: (i, off + j))
    halo = lambda off: pl.BlockSpec((8, tc), lambda j, i: (jnp.maximum(i * (tm // 8) - 1, 0), off + j))
    return pl.pallas_call(
        _conv_prompt_kernel,
        grid=(cw // tc, t // tm),
        in_specs=[main(ob), main(oc), main(oh), main(og), halo(oc), halo(oh),
                  pl.BlockSpec((conv_w.shape[0], tc), lambda j, i: (0, j))],
        out_specs=[pl.BlockSpec((tm, tc), lambda j, i: (i, j)),
                   pl.BlockSpec((8, tc), lambda j, i: (0, j))],
        out_shape=[jax.ShapeDtypeStruct((t, cw), _BF16), jax.ShapeDtypeStruct((8, cw), _F32)],
        compiler_params=_params("parallel", "arbitrary"),
        name="conv_prompt",
    )(rest, rest, rest, rest, rest, rest, conv_w)


def _conv_sample(rest, state, conv_w, cw, tn):
    m = rest.shape[0]
    b = m // tn
    assert tn >= 2 and conv_w.shape[0] == 3
    tc = _tile(cw, 512)
    ob, oc, oh, og = _rest_cols(cw, tc)
    zeros = jnp.zeros((b, tn - 1, cw), _F32)
    s1 = jnp.concatenate([state[:, 1:2], zeros], axis=1).reshape(m, cw)
    s2 = jnp.concatenate([state, zeros[:, 1:]], axis=1).reshape(m, cw)
    main = lambda off: pl.BlockSpec((m, tc), lambda j: (0, off + j))
    own = pl.BlockSpec((m, tc), lambda j: (0, j))
    return pl.pallas_call(
        functools.partial(_conv_sample_kernel, tn=tn),
        grid=(cw // tc,),
        in_specs=[main(ob), main(oc), main(oh), main(og), own, own,
                  pl.BlockSpec((conv_w.shape[0], tc), lambda j: (0, j))],
        out_specs=[own, own],
        out_shape=[jax.ShapeDtypeStruct((m, cw), _BF16), jax.ShapeDtypeStruct((m, cw), _F32)],
        compiler_params=_params("parallel"),
        name="conv_sample",
    )(rest, rest, rest, rest, s1, s2, conv_w)


def _merge_kernel(a_ref, c_ref, wa_ref, wc_ref, ma_ref, mc_ref, o_ref):
    ya = jnp.dot(a_ref[...], wa_ref[...], preferred_element_type=_F32)
    yc = jnp.dot(c_ref[...], wc_ref[...], preferred_element_type=_F32)
    o_ref[...] = (jax.nn.sigmoid(ma_ref[...]) * ya + jax.nn.sigmoid(mc_ref[...]) * yc).astype(o_ref.dtype)


def _merge_cast_kernel(a_ref, c_ref, wa_ref, wc_ref, ma_ref, mc_ref, o_ref, wab_ref, wcb_ref):
    @pl.when(pl.program_id(1) == 0)
    def _():
        wab_ref[...] = wa_ref[...].astype(wab_ref.dtype)
        wcb_ref[...] = wc_ref[...].astype(wcb_ref.dtype)

    _merge_kernel(a_ref, c_ref, wab_ref, wcb_ref, ma_ref, mc_ref, o_ref)


def _merge(a, c, wa, wc, rest, gate_col0, cast=False):
    m, ka = a.shape
    kc = c.shape[1]
    d = wa.shape[1]
    tm = _tile(m, 512)
    tn = _tile(math.gcd(d, gate_col0), 512 if cast else 1024)
    oa = gate_col0 // tn
    oc = (gate_col0 + d) // tn
    out_spec = pl.BlockSpec((tm, tn), lambda j, i: (i, j))
    out_shape = jax.ShapeDtypeStruct((m, d), _BF16)
    if cast:
        out_spec = [out_spec, pl.BlockSpec((ka, tn), lambda j, i: (0, j)), pl.BlockSpec((kc, tn), lambda j, i: (0, j))]
        out_shape = [out_shape, jax.ShapeDtypeStruct((ka, d), _BF16), jax.ShapeDtypeStruct((kc, d), _BF16)]
    return pl.pallas_call(
        _merge_cast_kernel if cast else _merge_kernel,
        grid=(d // tn, m // tm),
        in_specs=[pl.BlockSpec((tm, ka), lambda j, i: (i, 0)),
                  pl.BlockSpec((tm, kc), lambda j, i: (i, 0)),
                  pl.BlockSpec((ka, tn), lambda j, i: (0, j)),
                  pl.BlockSpec((kc, tn), lambda j, i: (0, j)),
                  pl.BlockSpec((tm, tn), lambda j, i: (i, oa + j)),
                  pl.BlockSpec((tm, tn), lambda j, i: (i, oc + j))],
        out_specs=out_spec,
        out_shape=out_shape,
        compiler_params=_params("parallel", "arbitrary"),
        name="merge",
    )(a, c, wa, wc, rest, rest)


def _out_kernel(mg_ref, w_ref, x_ref, g_ref, o_ref, *maybe_wb_ref, tn, nj):
    j = pl.program_id(1)
    if maybe_wb_ref:
        maybe_wb_ref[0][...] = w_ref[...].astype(maybe_wb_ref[0].dtype)
        w_ref = maybe_wb_ref[0]
    z = x_ref[...] + jnp.dot(mg_ref[...], w_ref[...], preferred_element_type=_F32)
    for jj in range(nj):
        @pl.when(j == jj)
        def _(jj=jj):
            o_ref[:, jj * tn:(jj + 1) * tn] = z

    @pl.when(j == nj - 1)
    def _():
        nr = math.gcd(o_ref.shape[0], NORM_ROWS)

        def norm_rows(r, _):
            rows = pl.ds(pl.multiple_of(r * nr, nr), nr)
            y = o_ref[rows, :]
            ms = jnp.mean(y * y, axis=-1, keepdims=True)
            o_ref[rows, :] = (y * lax.rsqrt(ms + EPS)) * g_ref[...]
            return 0

        lax.fori_loop(0, o_ref.shape[0] // nr, norm_rows, 0)


def _out(merged, w, x, gain, cast=False):
    m, d = x.shape
    tm = _tile(m, 512)
    tn = _tile(d, 256 if cast else 1024)
    nj = d // tn
    out_spec = pl.BlockSpec((tm, d), lambda i, j: (i, 0))
    out_shape = jax.ShapeDtypeStruct((m, d), _F32)
    if cast:
        assert m == tm
        out_spec = [out_spec, pl.BlockSpec((d, tn), lambda i, j: (0, j))]
        out_shape = [out_shape, jax.ShapeDtypeStruct((d, d), _BF16)]
    return pl.pallas_call(
        functools.partial(_out_kernel, tn=tn, nj=nj),
        grid=(m // tm, nj),
        in_specs=[pl.BlockSpec((tm, d), lambda i, j: (i, 0)),
                  pl.BlockSpec((d, tn), lambda i, j: (0, j)),
                  pl.BlockSpec((tm, tn), lambda i, j: (i, j)),
                  pl.BlockSpec((1, d), lambda i, j: (0, 0))],
        out_specs=out_spec,
        out_shape=out_shape,
        compiler_params=_params("parallel", "arbitrary"),
        name="out_proj",
    )(merged, w, x, gain.reshape(1, d))


def _layer(x, h, weights, out_gain, aw, cw, attend, conv, rest=None):
    w_in, wa, wc, w_out = weights
    cast = not isinstance(w_in, (list, tuple))
    names = ("proj_q", "proj_k", "proj_v", "proj_rest")
    if cast:
        col0s = (0, aw, 2 * aw, 3 * aw)
        widths = (aw, aw, aw, w_in.shape[1] - 3 * aw)
        outs = [_proj(h, w_in, c0, n, name, cast=True) for c0, n, name in zip(col0s, widths, names)]
        (q, k, v, rest), w_in = zip(*outs)
    else:
        q, k, v = (_proj(h, w, 0, w.shape[1], name) for w, name in zip(w_in[:3], names))
        if rest is None:
            rest = _proj(h, w_in[3], 0, w_in[3].shape[1], names[3])
    a = attend(q, k, v, rest, w_in)
    c, state = conv(rest)
    if cast:
        merged, wa, wc = _merge(a, c, wa, wc, rest, aw + 4 * cw, cast=True)
        y, w_out = _out(merged, w_out, x, out_gain, cast=True)
    else:
        y = _out(_merge(a, c, wa, wc, rest, aw + 4 * cw), w_out, x, out_gain)
    return y, k, v, state, (list(w_in), wa, wc, w_out)


def kernel(x_prompt, x_sample, cache_k, cache_v, state_conv, page_table, norm_gain, w_in,
           conv_w, w_attn_out, w_conv_out, w_out, rel_bias, final_gain):
    depth = w_in.shape[0]
    assert depth == 1, "final norm is fused into the (single) layer's output projection"
    bp, t, d = x_prompt.shape
    assert bp == 1
    db, tn, _ = x_sample.shape
    n_pool, page, n_heads, hd = cache_k.shape[1:]
    aw = n_heads * hd
    cw = conv_w.shape[2]
    assert aw == cw

    l = 0
    ck = cache_k.reshape(depth * n_pool, page * n_heads, hd)
    cv = cache_v.reshape(depth * n_pool, page * n_heads, hd)

    def attend_prompt(q, k, v, rest, w_groups):
        return _attn_prompt(q, k, v, rest, rel_bias, n_heads, hd)

    def conv_prompt(rest):
        c, st = _conv_prompt(rest, conv_w[l], cw)
        return c, st[8 - (conv_w.shape[1] - 1):]

    h_prompt = _rmsnorm_bf16(x_prompt.reshape(t, d), norm_gain[l])
    rest_prompt = []

    def attend_sample(q, k, v, rest, w_groups):
        r3 = lambda z: z.reshape(db, tn, z.shape[-1])
        a, rest_p = _attn_sample(r3(q), r3(k), r3(v), rest, ck, cv, page_table + l * n_pool, rel_bias,
                                 n_heads, hd, h_prompt, w_groups[3])
        rest_prompt.append(rest_p)
        return a

    def conv_sample(rest):
        c, u = _conv_sample(rest, state_conv[l], conv_w[l], cw, tn)
        return c, u.reshape(db, tn, cw)[:, tn - (conv_w.shape[1] - 1):]

    weights = (w_in[l], w_attn_out[l], w_conv_out[l], w_out[l])
    xs = x_sample.reshape(db * tn, d)
    ys, ks, vs, cs, weights_b = _layer(xs, _rmsnorm_bf16(xs, norm_gain[l]), weights,
                                       final_gain, aw, cw, attend_sample, conv_sample)
    yp, kp, vp, cp, _ = _layer(x_prompt.reshape(t, d), h_prompt, weights_b,
                               final_gain, aw, cw, attend_prompt, conv_prompt, rest=rest_prompt[0])

    return (yp.reshape(1, t, d), ys.reshape(db, tn, d),
            kp.reshape(1, 1, t, n_heads, hd), vp.reshape(1, 1, t, n_heads, hd),
            cp.reshape(1, 1, conv_w.shape[1] - 1, cw),
            ks.reshape(1, db, tn, n_heads, hd), vs.reshape(1, db, tn, n_heads, hd),
            cs.reshape(1, db, conv_w.shape[1] - 1, cw))
```

```python
import functools
import math

import numpy as np
import jax
import jax.numpy as jnp
from jax import lax
from jax.experimental import pallas as pl
from jax.experimental.pallas import tpu as pltpu

MOBA_BLOCK = 256
MOBA_TOPK = 3
N_BUCKETS = 32
MAX_DISTANCE = 128
EPS = 1e-6
MASKED = -1e30
LANES = 128
SUBLANES = 8
HALO_ROWS = 16
SUM_ROWS = 16
NORM_ROWS = 64
LOG2E = math.log2(math.e)
FAR_GROUP = 2
HEADS_PER_STEP = 2
PAGES_PER_STEP = 4
VMEM_LIMIT_BYTES = 56 * 1024 * 1024

_BF16 = jnp.bfloat16
_F32 = jnp.float32


def _params(*semantics):
    return pltpu.CompilerParams(dimension_semantics=semantics, vmem_limit_bytes=VMEM_LIMIT_BYTES)


def _tile(dim, want):
    t = min(dim, want)
    while dim % t:
        t -= LANES
        assert t > 0, (dim, want)
    return t


def _rmsnorm_kernel(x_ref, g_ref, o_ref):
    x = x_ref[...]
    ms = jnp.mean(x * x, axis=-1, keepdims=True)
    o_ref[...] = ((x * lax.rsqrt(ms + EPS)) * g_ref[...]).astype(o_ref.dtype)


def _rmsnorm_bf16(x, gain):
    m, d = x.shape
    tm = _tile(m, 256)
    return pl.pallas_call(
        _rmsnorm_kernel,
        grid=(m // tm,),
        in_specs=[pl.BlockSpec((tm, d), lambda i: (i, 0)), pl.BlockSpec((1, d), lambda i: (0, 0))],
        out_specs=pl.BlockSpec((tm, d), lambda i: (i, 0)),
        out_shape=jax.ShapeDtypeStruct((m, d), _BF16),
        compiler_params=_params("parallel"),
        name="rmsnorm",
    )(x, gain.reshape(1, d))


def _proj_kernel(h_ref, w_ref, o_ref):
    o_ref[...] = jnp.dot(h_ref[...], w_ref[...], preferred_element_type=_F32).astype(o_ref.dtype)


def _proj_cast_kernel(h_ref, w_ref, o_ref, wb_ref):
    @pl.when(pl.program_id(1) == 0)
    def _():
        wb_ref[...] = w_ref[...].astype(wb_ref.dtype)

    o_ref[...] = jnp.dot(h_ref[...], wb_ref[...], preferred_element_type=_F32).astype(o_ref.dtype)


def _proj(h, w, col0, ncols, name, cast=False):
    m, k = h.shape
    tm = _tile(m, 1024)
    tn = _tile(math.gcd(ncols, col0), 512 if cast else 1024)
    c0 = col0 // tn
    in_specs = [pl.BlockSpec((tm, k), lambda j, i: (i, 0)),
                pl.BlockSpec((k, tn), lambda j, i: (0, c0 + j))]
    out_spec = pl.BlockSpec((tm, tn), lambda j, i: (i, j))
    out_shape = jax.ShapeDtypeStruct((m, ncols), _F32)
    if cast:
        out_spec = [out_spec, pl.BlockSpec((k, tn), lambda j, i: (0, j))]
        out_shape = [out_shape, jax.ShapeDtypeStruct((k, ncols), _BF16)]
    return pl.pallas_call(
        _proj_cast_kernel if cast else _proj_kernel,
        grid=(ncols // tn, m // tm),
        in_specs=in_specs,
        out_specs=out_spec,
        out_shape=out_shape,
        compiler_params=_params("parallel", "arbitrary"),
        name=name,
    )(h, w)


def _t5_bucket(dist):
    max_exact = N_BUCKETS // 2
    n = jnp.maximum(dist, 0)
    nf = jnp.maximum(n, 1).astype(_F32)
    large = max_exact + (jnp.log(nf / max_exact) / math.log(MAX_DISTANCE / max_exact)
                         * (N_BUCKETS - max_exact)).astype(jnp.int32)
    large = jnp.minimum(large, N_BUCKETS - 1)
    return jnp.where(n < max_exact, n, large)


def _bias_table(rel_bias, n_dist):
    return rel_bias[_t5_bucket(jnp.arange(n_dist, dtype=jnp.int32))].T.astype(_F32)


def _toeplitz(w):
    h, two_n = w.shape
    n = two_n // 2
    flat = jnp.tile(w, (1, n))[:, :n * (two_n - 1)]
    return flat.reshape(h, n, two_n - 1)[:, :, :n]


def _topk_mask(gate, n_valid, axis):
    idx = lax.broadcasted_iota(jnp.int32, gate.shape, axis).astype(_F32)
    g = jnp.where(idx < n_valid, gate, -jnp.inf)
    sel = jnp.zeros(gate.shape, _F32)
    for _ in range(MOBA_TOPK):
        mx = jnp.max(g, axis=axis, keepdims=True)
        first = jnp.min(jnp.where(g == mx, idx, float(gate.shape[axis])), axis=axis, keepdims=True)
        pick = idx == jnp.where(mx > -jnp.inf, first, -1.0)
        sel = jnp.where(pick, 1.0, sel)
        g = jnp.where(pick, -jnp.inf, g)
    return sel


def _nt_dot(a, b, precision=None):
    return lax.dot_general(a, b, (((1,), (1,)), ((), ())), precision=precision, preferred_element_type=_F32)


def _attn_prompt_kernel(far_ref, q_ref, k_ref, v_ref, g_ref, bown_ref, bprev_ref, o_ref,
                        kb_ref, vt_ref, km_ref, sel_ref, s_ref, *, qscale):
    hp = pl.program_id(0)
    i = pl.program_id(1)
    n_h, nb, blk, hd = kb_ref.shape
    head_cols = [slice(e * hd, (e + 1) * hd) for e in range(n_h)]

    @pl.when(i == 0)
    def _():
        def load_block(c, _):
            rows = pl.ds(pl.multiple_of(c * blk, blk), blk)
            for e in range(n_h):
                kf = k_ref[rows, head_cols[e]]
                kb_ref[e, c] = kf.astype(_BF16)
                vt_ref[e, c, :hd, :] = v_ref[rows, head_cols[e]].T.astype(_BF16)
                extra = lax.broadcasted_iota(jnp.int32, (SUM_ROWS, blk), 0) == 0
                vt_ref[e, c, hd:, :] = jnp.where(extra, 1.0, 0.0).astype(_BF16)
                km_ref[e, pl.ds(c, 1), :] = jnp.mean(kf, axis=0, keepdims=True)
            return 0

        lax.fori_loop(0, nb, load_block, 0)

    def scores(e, j, qtb):
        return jnp.dot(kb_ref[e, j], qtb, preferred_element_type=_F32)

    def weighted_values(e, j, p):
        return jnp.dot(vt_ref[e, j], p.astype(_BF16), preferred_element_type=_F32)

    def far_block(gidx, g):
        return jnp.minimum(gidx * FAR_GROUP + g, nb - 1)

    def far_scores(buf, gidx):
        for e in range(n_h):
            for g in range(FAR_GROUP):
                s_ref[buf, e, g] = scores(e, far_block(gidx, g), qtbs[e])

    def far_bias_row(e, gidx, g):
        visible = gidx * FAR_GROUP + g < i - 1
        return jnp.where(sel_ref[e, pl.ds(far_block(gidx, g), 1), :] > 0.0,
                         jnp.where(visible, far_biases[e], MASKED), MASKED)

    def far_softmax(buf, gidx, carry):
        out = []
        for e, (m, acc) in enumerate(carry):
            pens = [far_bias_row(e, gidx, g) for g in range(FAR_GROUP)]
            m_new = m
            for g in range(FAR_GROUP):
                m_new = jnp.maximum(m_new, jnp.max(s_ref[buf, e, g], axis=0, keepdims=True) + pens[g])
            acc = jnp.exp2(m - m_new) * acc
            for g in range(FAR_GROUP):
                p = jnp.exp2(s_ref[buf, e, g] - (m_new - pens[g]))
                acc = acc + weighted_values(e, far_block(gidx, g), p)
            out.append((m_new, acc))
        return tuple(out)

    def split_bf16(x):
        hi = x.astype(_BF16)
        return hi, (x - hi.astype(_F32)).astype(_BF16)

    jp = jnp.maximum(i - 1, 0)
    far_biases = [far_ref[hp * n_h + e] for e in range(n_h)]
    qts = [q_ref[:, head_cols[e]].T for e in range(n_h)]
    qtbs = [(qt * qscale).astype(_BF16) for qt in qts]
    s_own = [scores(e, i, qtbs[e]) + bown_ref[e] for e in range(n_h)]
    gates = []
    for e in range(n_h):
        q_hi, q_lo = split_bf16(qts[e])
        km_hi, km_lo = split_bf16(km_ref[e])
        gates.append(jnp.dot(km_hi, q_hi, preferred_element_type=_F32)
                     + jnp.dot(km_lo, q_hi, preferred_element_type=_F32)
                     + jnp.dot(km_hi, q_lo, preferred_element_type=_F32))
    far_scores(0, 0)
    s_prev = [scores(e, jp, qtbs[e]) + bprev_ref[e] for e in range(n_h)]
    carry = []
    for e in range(n_h):
        m = jnp.max(s_own[e], axis=0, keepdims=True)
        acc = weighted_values(e, i, jnp.exp2(s_own[e] - m))
        sel_ref[e] = _topk_mask(gates[e], i.astype(_F32), axis=0)
        pen_prev = jnp.where(sel_ref[e, pl.ds(jp, 1), :] > 0.0, 0.0, MASKED)
        m_new = jnp.maximum(m, jnp.max(s_prev[e], axis=0, keepdims=True) + pen_prev)
        acc = jnp.exp2(m - m_new) * acc + weighted_values(e, jp, jnp.exp2(s_prev[e] - (m_new - pen_prev)))
        carry.append((m_new, acc))

    def far_pair(t, carry):
        far_scores(1, 2 * t + 1)
        carry = far_softmax(0, 2 * t, carry)
        far_scores(0, 2 * t + 2)
        return far_softmax(1, 2 * t + 1, carry)

    n_groups = (jnp.maximum(i - 1, 0) + FAR_GROUP - 1) // FAR_GROUP
    far_scores(0, 0)
    carry = lax.fori_loop(0, (n_groups + 1) // 2, far_pair, tuple(carry))
    for e, (m, acc) in enumerate(carry):
        g = g_ref[:, head_cols[e]].astype(_F32)
        out = acc[:hd, :] / acc[hd:hd + 1, :]
        o_ref[:, head_cols[e]] = (out.T * (g * jax.nn.sigmoid(g))).astype(o_ref.dtype)


def _attn_prompt(q, k, v, rest, rel_bias, n_heads, hd):
    t = q.shape[0]
    blk = MOBA_BLOCK
    assert t % blk == 0 and hd == LANES
    nb = t // blk
    assert blk + 1 >= MAX_DISTANCE
    tab = _bias_table(rel_bias, 2 * blk)
    bown = _toeplitz(jnp.concatenate([tab[:, :blk], jnp.full((n_heads, blk), MASKED, _F32)], axis=1)) * LOG2E
    bprev = _toeplitz(jnp.concatenate([tab[:, blk:], tab[:, :blk]], axis=1)) * LOG2E
    far = rel_bias[N_BUCKETS - 1].astype(_F32) * LOG2E
    n_h = math.gcd(n_heads, HEADS_PER_STEP)
    kernel = functools.partial(_attn_prompt_kernel, qscale=hd ** -0.5 * LOG2E)
    return pl.pallas_call(
        kernel,
        grid=(n_heads // n_h, nb),
        in_specs=[pl.BlockSpec(memory_space=pltpu.SMEM),
                  pl.BlockSpec((blk, n_h * hd), lambda h, i: (i, h)),
                  pl.BlockSpec((t, n_h * hd), lambda h, i: (0, h)),
                  pl.BlockSpec((t, n_h * hd), lambda h, i: (0, h)),
                  pl.BlockSpec((blk, n_h * hd), lambda h, i: (i, h)),
                  pl.BlockSpec((n_h, blk, blk), lambda h, i: (h, 0, 0)),
                  pl.BlockSpec((n_h, blk, blk), lambda h, i: (h, 0, 0))],
        out_specs=pl.BlockSpec((blk, n_h * hd), lambda h, i: (i, h)),
        out_shape=jax.ShapeDtypeStruct((t, n_heads * hd), _BF16),
        scratch_shapes=[pltpu.VMEM((n_h, nb, blk, hd), _BF16), pltpu.VMEM((n_h, nb, hd + SUM_ROWS, blk), _BF16),
                        pltpu.VMEM((n_h, nb, hd), _F32), pltpu.VMEM((n_h, nb, blk), _F32),
                        pltpu.VMEM((2, n_h, FAR_GROUP, blk, blk), _F32)],
        compiler_params=_params("parallel", "arbitrary"),
        name="attn_prompt",
    )(far, q, k, v, rest, bown, bprev)


def _sample_unit(j, i, n_i, n_steps, n_batch):
    step = j * n_i + i
    b = step // n_steps
    return jnp.minimum(b, n_batch - 1), jnp.where(b < n_batch, step % n_steps, n_steps - 1)


def _attn_sample_kernel(pt_ref, h_ref, w_ref, q_ref, qb_ref, *refs, n_heads, tn, pages_per_block, n_batch):
    pps = (len(refs) - 12) // 2
    k_refs, v_refs = refs[:pps], refs[pps:2 * pps]
    (bfar_ref, blast_ref, kn_ref, vn_ref, bnew_ref, g_ref, o_ref, orest_ref,
     m_ref, l_ref, pv_ref, ks_ref) = refs[2 * pps:]
    del pt_ref
    n_pages, rows, hd = pv_ref.shape
    nblk = ks_ref.shape[0] // n_heads
    n_steps = n_pages // pps
    bi, jj = _sample_unit(pl.program_id(0), pl.program_id(1), pl.num_programs(1), n_steps, n_batch)
    qb = qb_ref[...]
    mm_cols = w_ref.shape[1] // pps
    assert mm_cols % LANES == 0

    for s in range(pps):
        page = jj * pps + s
        kf = k_refs[s][...]
        bias = (blast_ref if s == pps - 1 else bfar_ref)[...]
        sc = _nt_dot(qb, kf.astype(_BF16)) + bias
        chunk = slice(s * mm_cols, (s + 1) * mm_cols)
        orest_ref[:, chunk] = jnp.dot(h_ref[...], w_ref[:, chunk],
                                      preferred_element_type=_F32).astype(orest_ref.dtype)
        mp = jnp.max(sc, axis=-1, keepdims=True)
        p = jnp.exp2(sc - mp)
        lp = jnp.sum(p, axis=-1, keepdims=True)
        pv_ref[page] = jnp.dot(p.astype(_BF16), v_refs[s][...].astype(_BF16), preferred_element_type=_F32)
        m_ref[page] = jnp.broadcast_to(mp, (rows, hd))
        l_ref[page] = jnp.broadcast_to(lp, (rows, hd))
        ksum = jnp.sum(kf.reshape(kf.shape[0] // n_heads, n_heads, hd), axis=0)
        blk_rows = pl.ds(pl.multiple_of((page // pages_per_block) * n_heads, n_heads), n_heads)
        if s % pages_per_block == 0:
            ks_ref[blk_rows, :] = ksum
        else:
            ks_ref[blk_rows, :] = ks_ref[blk_rows, :] + ksum

    @pl.when(jj == n_steps - 1)
    def _():
        q32 = q_ref[...]
        g2 = _nt_dot(q32, ks_ref[...], precision=lax.Precision.HIGHEST)
        row_head = lax.broadcasted_iota(jnp.int32, g2.shape, 0) // tn
        col_head = lax.broadcasted_iota(jnp.int32, g2.shape, 1) % n_heads
        own_head = jnp.where(col_head == row_head, g2, 0.0)
        fold = (lax.broadcasted_iota(jnp.int32, (g2.shape[1], LANES), 0) // n_heads
                == lax.broadcasted_iota(jnp.int32, (g2.shape[1], LANES), 1)).astype(_F32)
        gate = jnp.dot(own_head, fold, precision=lax.Precision.HIGHEST, preferred_element_type=_F32)
        sel = _topk_mask(gate, float(nblk), axis=1)

        s_new = _nt_dot(qb, kn_ref[...]) + bnew_ref[...]
        m_new = jnp.max(s_new, axis=-1, keepdims=True)
        live = [jnp.broadcast_to(sel[:, b:b + 1], (rows, hd)) > 0.0 for b in range(nblk)]
        m_tot = jnp.broadcast_to(m_new, (rows, hd))
        for pg in range(n_pages):
            m_tot = jnp.maximum(m_tot, jnp.where(live[pg // pages_per_block], m_ref[pg], MASKED))
        p_new = jnp.exp2(s_new - m_tot[:, :s_new.shape[1]])
        l = jnp.broadcast_to(jnp.sum(p_new, axis=-1, keepdims=True), (rows, hd))
        acc = jnp.dot(p_new.astype(_BF16), vn_ref[...], preferred_element_type=_F32)
        for pg in range(n_pages):
            w = jnp.where(live[pg // pages_per_block], jnp.exp2(m_ref[pg] - m_tot), 0.0)
            l = l + w * l_ref[pg]
            acc = acc + w * pv_ref[pg]
        out = acc / l
        for pos in range(SUBLANES // tn):
            @pl.when(bi % (SUBLANES // tn) == pos)
            def _(pos=pos):
                mine = slice(pos * tn, (pos + 1) * tn)
                g = g_ref[mine, :].astype(_F32)
                gated = g * jax.nn.sigmoid(g)
                for hh in range(n_heads):
                    cols = slice(hh * hd, (hh + 1) * hd)
                    o_ref[mine, cols] = (out[hh * tn:(hh + 1) * tn, :] * gated[:, cols]).astype(o_ref.dtype)


def _attn_sample(q, k_new, v_new, rest, cache_k, cache_v, page_table, rel_bias, n_heads, hd, h_other, w_other):
    b, tn, width = q.shape
    n_pages = page_table.shape[1]
    page = cache_k.shape[1] // n_heads
    blk = MOBA_BLOCK
    pps = math.gcd(n_pages, PAGES_PER_STEP)
    past = n_pages * page
    assert blk % page == 0 and past % blk == 0 and n_pages <= LANES
    assert hd == LANES and page + 1 >= MAX_DISTANCE
    ppb = blk // page
    assert pps % ppb == 0
    nblk = past // blk
    rows = n_heads * tn
    cols = page * n_heads

    qh = q.reshape(b, tn, n_heads, hd).transpose(0, 2, 1, 3).reshape(b, rows, hd)
    kn = k_new.reshape(b, tn * n_heads, hd).astype(_BF16)
    vn = v_new.reshape(b, tn * n_heads, hd).astype(_BF16)

    head_of_row = np.repeat(np.arange(n_heads), tn)
    t_of_row = np.tile(np.arange(tn), n_heads)
    diag = jnp.asarray(head_of_row[:, None] == np.tile(np.arange(n_heads), page)[None, :])
    tab = _bias_table(rel_bias, page + tn)
    far = rel_bias[N_BUCKETS - 1].astype(_F32)
    bias_far = jnp.where(diag, far[head_of_row][:, None], MASKED)
    d_last = page + np.arange(tn)[:, None] - np.arange(page)[None, :]
    b_last = tab[:, d_last]
    b_last = jnp.broadcast_to(b_last[:, :, :, None], (n_heads, tn, page, n_heads)).reshape(rows, cols)
    bias_pages = jnp.stack([bias_far, jnp.where(diag, b_last, MASKED)]) * LOG2E
    d_new = t_of_row[:, None] - np.repeat(np.arange(tn), n_heads)[None, :]
    same_head = head_of_row[:, None] == np.tile(np.arange(n_heads), tn)[None, :]
    b_new = tab[head_of_row[:, None], np.maximum(d_new, 0)]
    bias_new = jnp.where(jnp.asarray(same_head & (d_new >= 0)), b_new, MASKED) * LOG2E
    qb = (qh * (hd ** -0.5 * LOG2E)).astype(_BF16)

    n_steps = n_pages // pps
    m_o, k_o = h_other.shape
    n_o = w_other.shape[1]
    tm = _tile(m_o, 256)
    tno = max(c for c in range(pps * LANES, 1024 + 1, pps * LANES) if n_o % c == 0)
    n_j, n_i = n_o // tno, m_o // tm
    assert n_j * n_i >= b * n_steps
    kernel = functools.partial(_attn_sample_kernel, n_heads=n_heads, tn=tn, pages_per_block=ppb, n_batch=b)
    unit = lambda j, i: _sample_unit(j, i, n_i, n_steps, b)
    page_spec = lambda s: pl.BlockSpec(
        (None, cols, hd), lambda j, i, pt: (pt[unit(j, i)[0], unit(j, i)[1] * pps + s], 0, 0))
    per_b = lambda shape: pl.BlockSpec((None,) + shape, lambda j, i, pt: (unit(j, i)[0], 0, 0))
    assert SUBLANES % tn == 0 and (b * tn) % SUBLANES == 0
    shared_rows = pl.BlockSpec((SUBLANES, width), lambda j, i, pt: (unit(j, i)[0] * tn // SUBLANES, 0))
    grid_spec = pltpu.PrefetchScalarGridSpec(
        num_scalar_prefetch=1,
        grid=(n_j, n_i),
        in_specs=[pl.BlockSpec((tm, k_o), lambda j, i, pt: (i, 0)),
                  pl.BlockSpec((k_o, tno), lambda j, i, pt: (0, j)),
                  per_b((rows, hd)), per_b((rows, hd))]
                 + [page_spec(s) for s in range(pps)] + [page_spec(s) for s in range(pps)]
                 + [pl.BlockSpec((None, rows, cols), lambda j, i, pt: (0, 0, 0)),
                    pl.BlockSpec((None, rows, cols), lambda j, i, pt: ((unit(j, i)[1] + 1) // n_steps, 0, 0)),
                    per_b((tn * n_heads, hd)), per_b((tn * n_heads, hd)),
                    pl.BlockSpec((rows, tn * n_heads), lambda j, i, pt: (0, 0)),
                    shared_rows],
        out_specs=[shared_rows, pl.BlockSpec((tm, tno), lambda j, i, pt: (i, j))],
        scratch_shapes=[pltpu.VMEM((n_pages, rows, hd), _F32), pltpu.VMEM((n_pages, rows, hd), _F32),
                        pltpu.VMEM((n_pages, rows, hd), _F32), pltpu.VMEM((nblk * n_heads, hd), _F32)],
    )
    attn, other = pl.pallas_call(
        kernel,
        grid_spec=grid_spec,
        out_shape=[jax.ShapeDtypeStruct((b * tn, width), _F32), jax.ShapeDtypeStruct((m_o, n_o), _BF16)],
        compiler_params=_params("arbitrary", "arbitrary"),
        name="attn_sample_and_proj",
    )(page_table, h_other, w_other, qh, qb, *([cache_k] * pps), *([cache_v] * pps),
      bias_pages, bias_pages, kn, vn, bias_new, rest)
    return attn.astype(_BF16), other


def _conv_combine(u, prev1, prev2, w_ref, b_ref, g_ref, o_ref):
    y = w_ref[0:1, :] * prev2 + w_ref[1:2, :] * prev1 + w_ref[2:3, :] * u
    g = g_ref[...].astype(_F32)
    o_ref[...] = (b_ref[...].astype(_F32) * y * (g * jax.nn.sigmoid(g))).astype(o_ref.dtype)


def _conv_prompt_kernel(b_ref, c_ref, h_ref, g_ref, ch_ref, hh_ref, w_ref, o_ref, st_ref):
    i = pl.program_id(1)
    u = c_ref[...].astype(_F32) * h_ref[...].astype(_F32)
    halo = ch_ref[...].astype(_F32) * hh_ref[...].astype(_F32) * (i > 0).astype(_F32)
    last, before = halo[HALO_ROWS - 1:HALO_ROWS, :], halo[HALO_ROWS - 2:HALO_ROWS - 1, :]
    row = lax.broadcasted_iota(jnp.int32, u.shape, 0)
    prev1 = jnp.where(row == 0, last, pltpu.roll(u, 1, axis=0))
    prev2 = jnp.where(row == 0, before, jnp.where(row == 1, last, pltpu.roll(u, 2, axis=0)))
    _conv_combine(u, prev1, prev2, w_ref, b_ref, g_ref, o_ref)
    st_ref[...] = u[u.shape[0] - SUBLANES:, :]


def _conv_sample_kernel(b_ref, c_ref, h_ref, g_ref, s1_ref, s2_ref, w_ref, o_ref, u_ref, *, tn):
    u = c_ref[...].astype(_F32) * h_ref[...].astype(_F32)
    t = lax.broadcasted_iota(jnp.int32, u.shape, 0) % tn
    prev1 = jnp.where(t >= 1, pltpu.roll(u, 1, axis=0), s1_ref[...])
    prev2 = jnp.where(t >= 2, pltpu.roll(u, 2, axis=0), s2_ref[...])
    _conv_combine(u, prev1, prev2, w_ref, b_ref, g_ref, o_ref)
    u_ref[...] = u


def _rest_cols(cw, tc):
    return [(1 + n) * cw // tc for n in range(4)]


def _conv_prompt(rest, conv_w, cw):
    t = rest.shape[0]
    tm = _tile(t, 512)
    tc = _tile(cw, 512)
    ob, oc, oh, og = _rest_cols(cw, tc)
    main = lambda off: pl.BlockSpec((tm, tc), lambda j, i: (i, off + j))
    halo = lambda off: pl.BlockSpec((HALO_ROWS, tc),
                                    lambda j, i: (jnp.maximum(i * (tm // HALO_ROWS) - 1, 0), off + j))
    return pl.pallas_call(
        _conv_prompt_kernel,
        grid=(cw // tc, t // tm),
        in_specs=[main(ob), main(oc), main(oh), main(og), halo(oc), halo(oh),
                  pl.BlockSpec((conv_w.shape[0], tc), lambda j, i: (0, j))],
        out_specs=[pl.BlockSpec((tm, tc), lambda j, i: (i, j)),
                   pl.BlockSpec((8, tc), lambda j, i: (0, j))],
        out_shape=[jax.ShapeDtypeStruct((t, cw), _BF16), jax.ShapeDtypeStruct((8, cw), _F32)],
        compiler_params=_params("parallel", "arbitrary"),
        name="conv_prompt",
    )(rest, rest, rest, rest, rest, rest, conv_w)


def _conv_sample(rest, state, conv_w, cw, tn):
    m = rest.shape[0]
    b = m // tn
    assert tn >= 2 and conv_w.shape[0] == 3
    tc = _tile(cw, 512)
    ob, oc, oh, og = _rest_cols(cw, tc)
    zeros = jnp.zeros((b, tn - 1, cw), _F32)
    s1 = jnp.concatenate([state[:, 1:2], zeros], axis=1).reshape(m, cw)
    s2 = jnp.concatenate([state, zeros[:, 1:]], axis=1).reshape(m, cw)
    main = lambda off: pl.BlockSpec((m, tc), lambda j: (0, off + j))
    own = pl.BlockSpec((m, tc), lambda j: (0, j))
    return pl.pallas_call(
        functools.partial(_conv_sample_kernel, tn=tn),
        grid=(cw // tc,),
        in_specs=[main(ob), main(oc), main(oh), main(og), own, own,
                  pl.BlockSpec((conv_w.shape[0], tc), lambda j: (0, j))],
        out_specs=[own, own],
        out_shape=[jax.ShapeDtypeStruct((m, cw), _BF16), jax.ShapeDtypeStruct((m, cw), _F32)],
        compiler_params=_params("parallel"),
        name="conv_sample",
    )(rest, rest, rest, rest, s1, s2, conv_w)


def _merge_kernel(a_ref, c_ref, wa_ref, wc_ref, ma_ref, mc_ref, o_ref):
    ya = jnp.dot(a_ref[...], wa_ref[...], preferred_element_type=_F32)
    yc = jnp.dot(c_ref[...], wc_ref[...], preferred_element_type=_F32)
    gate_a = jax.nn.sigmoid(ma_ref[...].astype(_F32))
    gate_c = jax.nn.sigmoid(mc_ref[...].astype(_F32))
    o_ref[...] = (gate_a * ya + gate_c * yc).astype(o_ref.dtype)


def _merge_cast_kernel(a_ref, c_ref, wa_ref, wc_ref, ma_ref, mc_ref, o_ref, wab_ref, wcb_ref):
    @pl.when(pl.program_id(1) == 0)
    def _():
        wab_ref[...] = wa_ref[...].astype(wab_ref.dtype)
        wcb_ref[...] = wc_ref[...].astype(wcb_ref.dtype)

    _merge_kernel(a_ref, c_ref, wab_ref, wcb_ref, ma_ref, mc_ref, o_ref)


def _merge(a, c, wa, wc, rest, gate_col0, cast=False):
    m, ka = a.shape
    kc = c.shape[1]
    d = wa.shape[1]
    tm = _tile(m, 512)
    tn = _tile(math.gcd(d, gate_col0), 512 if cast else 1024)
    oa = gate_col0 // tn
    oc = (gate_col0 + d) // tn
    out_spec = pl.BlockSpec((tm, tn), lambda j, i: (i, j))
    out_shape = jax.ShapeDtypeStruct((m, d), _BF16)
    if cast:
        out_spec = [out_spec, pl.BlockSpec((ka, tn), lambda j, i: (0, j)), pl.BlockSpec((kc, tn), lambda j, i: (0, j))]
        out_shape = [out_shape, jax.ShapeDtypeStruct((ka, d), _BF16), jax.ShapeDtypeStruct((kc, d), _BF16)]
    return pl.pallas_call(
        _merge_cast_kernel if cast else _merge_kernel,
        grid=(d // tn, m // tm),
        in_specs=[pl.BlockSpec((tm, ka), lambda j, i: (i, 0)),
                  pl.BlockSpec((tm, kc), lambda j, i: (i, 0)),
                  pl.BlockSpec((ka, tn), lambda j, i: (0, j)),
                  pl.BlockSpec((kc, tn), lambda j, i: (0, j)),
                  pl.BlockSpec((tm, tn), lambda j, i: (i, oa + j)),
                  pl.BlockSpec((tm, tn), lambda j, i: (i, oc + j))],
        out_specs=out_spec,
        out_shape=out_shape,
        compiler_params=_params("parallel", "arbitrary"),
        name="merge",
    )(a, c, wa, wc, rest, rest)


def _out_kernel(mg_ref, w_ref, x_ref, g_ref, o_ref, *maybe_wb_ref, tn, nj):
    j = pl.program_id(1)
    if maybe_wb_ref:
        maybe_wb_ref[0][...] = w_ref[...].astype(maybe_wb_ref[0].dtype)
        w_ref = maybe_wb_ref[0]
    z = x_ref[...] + jnp.dot(mg_ref[...], w_ref[...], preferred_element_type=_F32)
    for jj in range(nj):
        @pl.when(j == jj)
        def _(jj=jj):
            o_ref[:, jj * tn:(jj + 1) * tn] = z

    @pl.when(j == nj - 1)
    def _():
        nr = math.gcd(o_ref.shape[0], NORM_ROWS)

        def norm_rows(r, _):
            rows = pl.ds(pl.multiple_of(r * nr, nr), nr)
            y = o_ref[rows, :]
            ms = jnp.mean(y * y, axis=-1, keepdims=True)
            o_ref[rows, :] = (y * lax.rsqrt(ms + EPS)) * g_ref[...]
            return 0

        lax.fori_loop(0, o_ref.shape[0] // nr, norm_rows, 0)


def _out(merged, w, x, gain, cast=False):
    m, d = x.shape
    tm = _tile(m, 512)
    tn = _tile(d, 256 if cast else 1024)
    nj = d // tn
    out_spec = pl.BlockSpec((tm, d), lambda i, j: (i, 0))
    out_shape = jax.ShapeDtypeStruct((m, d), _F32)
    if cast:
        assert m == tm
        out_spec = [out_spec, pl.BlockSpec((d, tn), lambda i, j: (0, j))]
        out_shape = [out_shape, jax.ShapeDtypeStruct((d, d), _BF16)]
    return pl.pallas_call(
        functools.partial(_out_kernel, tn=tn, nj=nj),
        grid=(m // tm, nj),
        in_specs=[pl.BlockSpec((tm, d), lambda i, j: (i, 0)),
                  pl.BlockSpec((d, tn), lambda i, j: (0, j)),
                  pl.BlockSpec((tm, tn), lambda i, j: (i, j)),
                  pl.BlockSpec((1, d), lambda i, j: (0, 0))],
        out_specs=out_spec,
        out_shape=out_shape,
        compiler_params=_params("parallel", "arbitrary"),
        name="out_proj",
    )(merged, w, x, gain.reshape(1, d))


def _layer(x, h, weights, out_gain, aw, cw, attend, conv, rest=None):
    w_in, wa, wc, w_out = weights
    cast = not isinstance(w_in, (list, tuple))
    names = ("proj_q", "proj_k", "proj_v", "proj_rest")
    if cast:
        col0s = (0, aw, 2 * aw, 3 * aw)
        widths = (aw, aw, aw, w_in.shape[1] - 3 * aw)
        outs = [_proj(h, w_in, c0, n, name, cast=True) for c0, n, name in zip(col0s, widths, names)]
        (q, k, v, rest), w_in = zip(*outs)
    else:
        q, k, v = (_proj(h, w, 0, w.shape[1], name) for w, name in zip(w_in[:3], names))
        if rest is None:
            rest = _proj(h, w_in[3], 0, w_in[3].shape[1], names[3])
    a = attend(q, k, v, rest, w_in)
    c, state = conv(rest)
    if cast:
        merged, wa, wc = _merge(a, c, wa, wc, rest, aw + 4 * cw, cast=True)
        y, w_out = _out(merged, w_out, x, out_gain, cast=True)
    else:
        y = _out(_merge(a, c, wa, wc, rest, aw + 4 * cw), w_out, x, out_gain)
    return y, k, v, state, (list(w_in), wa, wc, w_out)


def kernel(x_prompt, x_sample, cache_k, cache_v, state_conv, page_table, norm_gain, w_in,
           conv_w, w_attn_out, w_conv_out, w_out, rel_bias, final_gain):
    depth = w_in.shape[0]
    assert depth == 1, "final norm is fused into the (single) layer's output projection"
    bp, t, d = x_prompt.shape
    assert bp == 1
    db, tn, _ = x_sample.shape
    n_pool, page, n_heads, hd = cache_k.shape[1:]
    aw = n_heads * hd
    cw = conv_w.shape[2]
    assert aw == cw

    l = 0
    ck = cache_k.reshape(depth * n_pool, page * n_heads, hd)
    cv = cache_v.reshape(depth * n_pool, page * n_heads, hd)

    def attend_prompt(q, k, v, rest, w_groups):
        return _attn_prompt(q, k, v, rest, rel_bias, n_heads, hd)

    def conv_prompt(rest):
        c, st = _conv_prompt(rest, conv_w[l], cw)
        return c, st[8 - (conv_w.shape[1] - 1):]

    h_prompt = _rmsnorm_bf16(x_prompt.reshape(t, d), norm_gain[l])
    rest_prompt = []

    def attend_sample(q, k, v, rest, w_groups):
        r3 = lambda z: z.reshape(db, tn, z.shape[-1])
        a, rest_p = _attn_sample(r3(q), r3(k), r3(v), rest, ck, cv, page_table + l * n_pool, rel_bias,
                                 n_heads, hd, h_prompt, w_groups[3])
        rest_prompt.append(rest_p)
        return a

    def conv_sample(rest):
        c, u = _conv_sample(rest, state_conv[l], conv_w[l], cw, tn)
        return c, u.reshape(db, tn, cw)[:, tn - (conv_w.shape[1] - 1):]

    weights = (w_in[l], w_attn_out[l], w_conv_out[l], w_out[l])
    xs = x_sample.reshape(db * tn, d)
    ys, ks, vs, cs, weights_b = _layer(xs, _rmsnorm_bf16(xs, norm_gain[l]), weights,
                                       final_gain, aw, cw, attend_sample, conv_sample)
    yp, kp, vp, cp, _ = _layer(x_prompt.reshape(t, d), h_prompt, weights_b,
                               final_gain, aw, cw, attend_prompt, conv_prompt, rest=rest_prompt[0])

    return (yp.reshape(1, t, d), ys.reshape(db, tn, d),
            kp.reshape(1, 1, t, n_heads, hd), vp.reshape(1, 1, t, n_heads, hd),
            cp.reshape(1, 1, conv_w.shape[1] - 1, cw),
            ks.reshape(1, db, tn, n_heads, hd), vs.reshape(1, db, tn, n_heads, hd),
            cs.reshape(1, db, conv_w.shape[1] - 1, cw))
```

```python
import functools
import math

import numpy as np
import jax
import jax.numpy as jnp
from jax import lax
from jax.experimental import pallas as pl
from jax.experimental.pallas import tpu as pltpu

MOBA_BLOCK = 256
MOBA_TOPK = 3
N_BUCKETS = 32
MAX_DISTANCE = 128
EPS = 1e-6
MASKED = -1e30
LANES = 128
SUBLANES = 8
HALO_ROWS = 16
SUM_ROWS = 16
NORM_ROWS = 64
LOG2E = math.log2(math.e)
FAR_GROUP = 2
HEADS_PER_STEP = 2
MM_CHUNK_COLS = 256
PAGES_PER_STEP = 8
VMEM_LIMIT_BYTES = 56 * 1024 * 1024

_BF16 = jnp.bfloat16
_F32 = jnp.float32


def _params(*semantics):
    return pltpu.CompilerParams(dimension_semantics=semantics, vmem_limit_bytes=VMEM_LIMIT_BYTES)


def _tile(dim, want):
    t = min(dim, want)
    while dim % t:
        t -= LANES
        assert t > 0, (dim, want)
    return t


def _rmsnorm_kernel(x_ref, g_ref, o_ref):
    x = x_ref[...]
    ms = jnp.mean(x * x, axis=-1, keepdims=True)
    o_ref[...] = ((x * lax.rsqrt(ms + EPS)) * g_ref[...]).astype(o_ref.dtype)


def _rmsnorm_bf16(x, gain):
    m, d = x.shape
    tm = _tile(m, 256)
    return pl.pallas_call(
        _rmsnorm_kernel,
        grid=(m // tm,),
        in_specs=[pl.BlockSpec((tm, d), lambda i: (i, 0)), pl.BlockSpec((1, d), lambda i: (0, 0))],
        out_specs=pl.BlockSpec((tm, d), lambda i: (i, 0)),
        out_shape=jax.ShapeDtypeStruct((m, d), _BF16),
        compiler_params=_params("parallel"),
        name="rmsnorm",
    )(x, gain.reshape(1, d))


def _proj_kernel(h_ref, w_ref, o_ref):
    o_ref[...] = jnp.dot(h_ref[...], w_ref[...], preferred_element_type=_F32).astype(o_ref.dtype)


def _proj_cast_kernel(h_ref, w_ref, o_ref, wb_ref):
    @pl.when(pl.program_id(1) == 0)
    def _():
        wb_ref[...] = w_ref[...].astype(wb_ref.dtype)

    o_ref[...] = jnp.dot(h_ref[...], wb_ref[...], preferred_element_type=_F32).astype(o_ref.dtype)


def _proj(h, w, col0, ncols, name, cast=False):
    m, k = h.shape
    tm = _tile(m, 1024)
    tn = _tile(math.gcd(ncols, col0), 512 if cast else 1024)
    c0 = col0 // tn
    in_specs = [pl.BlockSpec((tm, k), lambda j, i: (i, 0)),
                pl.BlockSpec((k, tn), lambda j, i: (0, c0 + j))]
    out_spec = pl.BlockSpec((tm, tn), lambda j, i: (i, j))
    out_shape = jax.ShapeDtypeStruct((m, ncols), _F32)
    if cast:
        out_spec = [out_spec, pl.BlockSpec((k, tn), lambda j, i: (0, j))]
        out_shape = [out_shape, jax.ShapeDtypeStruct((k, ncols), _BF16)]
    return pl.pallas_call(
        _proj_cast_kernel if cast else _proj_kernel,
        grid=(ncols // tn, m // tm),
        in_specs=in_specs,
        out_specs=out_spec,
        out_shape=out_shape,
        compiler_params=_params("parallel", "arbitrary"),
        name=name,
    )(h, w)


def _t5_bucket(dist):
    max_exact = N_BUCKETS // 2
    n = jnp.maximum(dist, 0)
    nf = jnp.maximum(n, 1).astype(_F32)
    large = max_exact + (jnp.log(nf / max_exact) / math.log(MAX_DISTANCE / max_exact)
                         * (N_BUCKETS - max_exact)).astype(jnp.int32)
    large = jnp.minimum(large, N_BUCKETS - 1)
    return jnp.where(n < max_exact, n, large)


def _bias_table(rel_bias, n_dist):
    return rel_bias[_t5_bucket(jnp.arange(n_dist, dtype=jnp.int32))].T.astype(_F32)


def _toeplitz(w):
    h, two_n = w.shape
    n = two_n // 2
    flat = jnp.tile(w, (1, n))[:, :n * (two_n - 1)]
    return flat.reshape(h, n, two_n - 1)[:, :, :n]


def _topk_mask(gate, n_valid, axis):
    idx = lax.broadcasted_iota(jnp.int32, gate.shape, axis).astype(_F32)
    g = jnp.where(idx < n_valid, gate, -jnp.inf)
    sel = jnp.zeros(gate.shape, _F32)
    for _ in range(MOBA_TOPK):
        mx = jnp.max(g, axis=axis, keepdims=True)
        first = jnp.min(jnp.where(g == mx, idx, float(gate.shape[axis])), axis=axis, keepdims=True)
        pick = idx == jnp.where(mx > -jnp.inf, first, -1.0)
        sel = jnp.where(pick, 1.0, sel)
        g = jnp.where(pick, -jnp.inf, g)
    return sel


def _nt_dot(a, b, precision=None):
    return lax.dot_general(a, b, (((1,), (1,)), ((), ())), precision=precision, preferred_element_type=_F32)


def _attn_prompt_kernel(far_ref, q_ref, k_ref, v_ref, g_ref, bown_ref, bprev_ref, o_ref,
                        kb_ref, vt_ref, km_ref, sel_ref, s_ref, *, qscale):
    hp = pl.program_id(0)
    i = pl.program_id(1)
    n_h, nb, blk, hd = kb_ref.shape
    head_cols = [slice(e * hd, (e + 1) * hd) for e in range(n_h)]

    @pl.when(i == 0)
    def _():
        def load_block(c, _):
            rows = pl.ds(pl.multiple_of(c * blk, blk), blk)
            for e in range(n_h):
                kf = k_ref[rows, head_cols[e]]
                kb_ref[e, c] = kf.astype(_BF16)
                vt_ref[e, c, :hd, :] = v_ref[rows, head_cols[e]].T.astype(_BF16)
                extra = lax.broadcasted_iota(jnp.int32, (SUM_ROWS, blk), 0) == 0
                vt_ref[e, c, hd:, :] = jnp.where(extra, 1.0, 0.0).astype(_BF16)
                km_ref[e, pl.ds(c, 1), :] = jnp.mean(kf, axis=0, keepdims=True)
            return 0

        lax.fori_loop(0, nb, load_block, 0)

    def scores(e, j, qtb):
        return jnp.dot(kb_ref[e, j], qtb, preferred_element_type=_F32)

    def weighted_values(e, j, p):
        return jnp.dot(vt_ref[e, j], p.astype(_BF16), preferred_element_type=_F32)

    def far_block(gidx, g):
        return jnp.minimum(gidx * FAR_GROUP + g, nb - 1)

    def far_scores(buf, gidx):
        for e in range(n_h):
            for g in range(FAR_GROUP):
                s_ref[buf, e, g] = scores(e, far_block(gidx, g), qtbs[e])

    def far_bias_row(e, gidx, g):
        visible = gidx * FAR_GROUP + g < i - 1
        return jnp.where(sel_ref[e, pl.ds(far_block(gidx, g), 1), :] > 0.0,
                         jnp.where(visible, far_biases[e], MASKED), MASKED)

    def far_softmax(buf, gidx, carry):
        out = []
        for e, (m, acc) in enumerate(carry):
            pens = [far_bias_row(e, gidx, g) for g in range(FAR_GROUP)]
            m_new = m
            for g in range(FAR_GROUP):
                m_new = jnp.maximum(m_new, jnp.max(s_ref[buf, e, g], axis=0, keepdims=True) + pens[g])
            acc = jnp.exp2(m - m_new) * acc
            for g in range(FAR_GROUP):
                p = jnp.exp2(s_ref[buf, e, g] - (m_new - pens[g]))
                acc = acc + weighted_values(e, far_block(gidx, g), p)
            out.append((m_new, acc))
        return tuple(out)

    def split_bf16(x):
        hi = x.astype(_BF16)
        return hi, (x - hi.astype(_F32)).astype(_BF16)

    jp = jnp.maximum(i - 1, 0)
    far_biases = [far_ref[hp * n_h + e] for e in range(n_h)]
    qts = [q_ref[:, head_cols[e]].T for e in range(n_h)]
    qtbs = [(qt * qscale).astype(_BF16) for qt in qts]
    s_own = [scores(e, i, qtbs[e]) + bown_ref[e] for e in range(n_h)]
    gates = []
    for e in range(n_h):
        q_hi, q_lo = split_bf16(qts[e])
        km_hi, km_lo = split_bf16(km_ref[e])
        gates.append(jnp.dot(km_hi, q_hi, preferred_element_type=_F32)
                     + jnp.dot(km_lo, q_hi, preferred_element_type=_F32)
                     + jnp.dot(km_hi, q_lo, preferred_element_type=_F32))
    far_scores(0, 0)
    s_prev = [scores(e, jp, qtbs[e]) + bprev_ref[e] for e in range(n_h)]
    carry = []
    for e in range(n_h):
        m = jnp.max(s_own[e], axis=0, keepdims=True)
        acc = weighted_values(e, i, jnp.exp2(s_own[e] - m))
        sel_ref[e] = _topk_mask(gates[e], i.astype(_F32), axis=0)
        pen_prev = jnp.where(sel_ref[e, pl.ds(jp, 1), :] > 0.0, 0.0, MASKED)
        m_new = jnp.maximum(m, jnp.max(s_prev[e], axis=0, keepdims=True) + pen_prev)
        acc = jnp.exp2(m - m_new) * acc + weighted_values(e, jp, jnp.exp2(s_prev[e] - (m_new - pen_prev)))
        carry.append((m_new, acc))

    def far_pair(t, carry):
        far_scores(1, 2 * t + 1)
        carry = far_softmax(0, 2 * t, carry)
        far_scores(0, 2 * t + 2)
        return far_softmax(1, 2 * t + 1, carry)

    n_groups = (jnp.maximum(i - 1, 0) + FAR_GROUP - 1) // FAR_GROUP
    far_scores(0, 0)
    carry = lax.fori_loop(0, (n_groups + 1) // 2, far_pair, tuple(carry))
    for e, (m, acc) in enumerate(carry):
        g = g_ref[:, head_cols[e]].astype(_F32)
        out = acc[:hd, :] / acc[hd:hd + 1, :]
        o_ref[:, head_cols[e]] = (out.T * (g * jax.nn.sigmoid(g))).astype(o_ref.dtype)


def _attn_prompt(q, k, v, rest, rel_bias, n_heads, hd):
    t = q.shape[0]
    blk = MOBA_BLOCK
    assert t % blk == 0 and hd == LANES
    nb = t // blk
    assert blk + 1 >= MAX_DISTANCE
    tab = _bias_table(rel_bias, 2 * blk)
    bown = _toeplitz(jnp.concatenate([tab[:, :blk], jnp.full((n_heads, blk), MASKED, _F32)], axis=1)) * LOG2E
    bprev = _toeplitz(jnp.concatenate([tab[:, blk:], tab[:, :blk]], axis=1)) * LOG2E
    far = rel_bias[N_BUCKETS - 1].astype(_F32) * LOG2E
    n_h = math.gcd(n_heads, HEADS_PER_STEP)
    kernel = functools.partial(_attn_prompt_kernel, qscale=hd ** -0.5 * LOG2E)
    return pl.pallas_call(
        kernel,
        grid=(n_heads // n_h, nb),
        in_specs=[pl.BlockSpec(memory_space=pltpu.SMEM),
                  pl.BlockSpec((blk, n_h * hd), lambda h, i: (i, h)),
                  pl.BlockSpec((t, n_h * hd), lambda h, i: (0, h)),
                  pl.BlockSpec((t, n_h * hd), lambda h, i: (0, h)),
                  pl.BlockSpec((blk, n_h * hd), lambda h, i: (i, h)),
                  pl.BlockSpec((n_h, blk, blk), lambda h, i: (h, 0, 0)),
                  pl.BlockSpec((n_h, blk, blk), lambda h, i: (h, 0, 0))],
        out_specs=pl.BlockSpec((blk, n_h * hd), lambda h, i: (i, h)),
        out_shape=jax.ShapeDtypeStruct((t, n_heads * hd), _BF16),
        scratch_shapes=[pltpu.VMEM((n_h, nb, blk, hd), _BF16), pltpu.VMEM((n_h, nb, hd + SUM_ROWS, blk), _BF16),
                        pltpu.VMEM((n_h, nb, hd), _F32), pltpu.VMEM((n_h, nb, blk), _F32),
                        pltpu.VMEM((2, n_h, FAR_GROUP, blk, blk), _F32)],
        compiler_params=_params("parallel", "arbitrary"),
        name="attn_prompt",
    )(far, q, k, v, rest, bown, bprev)


def _sample_unit(j, i, n_i, n_steps, n_batch):
    step = j * n_i + i
    b = step // n_steps
    return jnp.minimum(b, n_batch - 1), jnp.where(b < n_batch, step % n_steps, n_steps - 1)


def _attn_sample_kernel(pt_ref, h_ref, w_ref, q_ref, qb_ref, *refs, n_heads, tn, pages_per_block, n_batch):
    pps = (len(refs) - 12) // 2
    k_refs, v_refs = refs[:pps], refs[pps:2 * pps]
    (bfar_ref, blast_ref, kn_ref, vn_ref, bnew_ref, g_ref, o_ref, orest_ref,
     m_ref, l_ref, pv_ref, ks_ref) = refs[2 * pps:]
    del pt_ref
    n_pages, rows, hd = pv_ref.shape
    nblk = ks_ref.shape[0] // n_heads
    n_steps = n_pages // pps
    bi, jj = _sample_unit(pl.program_id(0), pl.program_id(1), pl.num_programs(1), n_steps, n_batch)
    qb = qb_ref[...]
    n_chunks = w_ref.shape[1] // MM_CHUNK_COLS
    assert w_ref.shape[1] % MM_CHUNK_COLS == 0 and pps % n_chunks == 0
    group = pps // n_chunks
    for c in range(n_chunks):
        members = range(c * group, (c + 1) * group)
        kfs = {s: k_refs[s][...] for s in members}
        scs = {s: _nt_dot(qb, kfs[s].astype(_BF16)) + (blast_ref if s == pps - 1 else bfar_ref)[...]
               for s in members}
        chunk = slice(c * MM_CHUNK_COLS, (c + 1) * MM_CHUNK_COLS)
        orest_ref[:, chunk] = jnp.dot(h_ref[...], w_ref[:, chunk],
                                      preferred_element_type=_F32).astype(orest_ref.dtype)
        for s in members:
            page = jj * pps + s
            mp = jnp.max(scs[s], axis=-1, keepdims=True)
            p = jnp.exp2(scs[s] - mp)
            lp = jnp.sum(p, axis=-1, keepdims=True)
            pv_ref[page] = jnp.dot(p.astype(_BF16), v_refs[s][...].astype(_BF16), preferred_element_type=_F32)
            m_ref[page] = jnp.broadcast_to(mp, (rows, hd))
            l_ref[page] = jnp.broadcast_to(lp, (rows, hd))
            ksum = jnp.sum(kfs[s].reshape(kfs[s].shape[0] // n_heads, n_heads, hd), axis=0)
            blk_rows = pl.ds(pl.multiple_of((page // pages_per_block) * n_heads, n_heads), n_heads)
            if s % pages_per_block == 0:
                ks_ref[blk_rows, :] = ksum
            else:
                ks_ref[blk_rows, :] = ks_ref[blk_rows, :] + ksum

    @pl.when(jj == n_steps - 1)
    def _():
        q32 = q_ref[...]
        g2 = _nt_dot(q32, ks_ref[...], precision=lax.Precision.HIGHEST)
        row_head = lax.broadcasted_iota(jnp.int32, g2.shape, 0) // tn
        col_head = lax.broadcasted_iota(jnp.int32, g2.shape, 1) % n_heads
        own_head = jnp.where(col_head == row_head, g2, 0.0)
        fold = (lax.broadcasted_iota(jnp.int32, (g2.shape[1], LANES), 0) // n_heads
                == lax.broadcasted_iota(jnp.int32, (g2.shape[1], LANES), 1)).astype(_F32)
        gate = jnp.dot(own_head, fold, precision=lax.Precision.HIGHEST, preferred_element_type=_F32)
        sel = _topk_mask(gate, float(nblk), axis=1)

        s_new = _nt_dot(qb, kn_ref[...]) + bnew_ref[...]
        m_new = jnp.max(s_new, axis=-1, keepdims=True)
        live = [jnp.broadcast_to(sel[:, b:b + 1], (rows, hd)) > 0.0 for b in range(nblk)]
        m_tot = jnp.broadcast_to(m_new, (rows, hd))
        for pg in range(n_pages):
            m_tot = jnp.maximum(m_tot, jnp.where(live[pg // pages_per_block], m_ref[pg], MASKED))
        p_new = jnp.exp2(s_new - m_tot[:, :s_new.shape[1]])
        l = jnp.broadcast_to(jnp.sum(p_new, axis=-1, keepdims=True), (rows, hd))
        acc = jnp.dot(p_new.astype(_BF16), vn_ref[...], preferred_element_type=_F32)
        for pg in range(n_pages):
            w = jnp.where(live[pg // pages_per_block], jnp.exp2(m_ref[pg] - m_tot), 0.0)
            l = l + w * l_ref[pg]
            acc = acc + w * pv_ref[pg]
        out = acc / l
        for pos in range(SUBLANES // tn):
            @pl.when(bi % (SUBLANES // tn) == pos)
            def _(pos=pos):
                mine = slice(pos * tn, (pos + 1) * tn)
                g = g_ref[mine, :].astype(_F32)
                gated = g * jax.nn.sigmoid(g)
                for hh in range(n_heads):
                    cols = slice(hh * hd, (hh + 1) * hd)
                    o_ref[mine, cols] = (out[hh * tn:(hh + 1) * tn, :] * gated[:, cols]).astype(o_ref.dtype)


def _attn_sample(q, k_new, v_new, rest, cache_k, cache_v, page_table, rel_bias, n_heads, hd, h_other, w_other):
    b, tn, width = q.shape
    n_pages = page_table.shape[1]
    page = cache_k.shape[1] // n_heads
    blk = MOBA_BLOCK
    pps = math.gcd(n_pages, PAGES_PER_STEP)
    past = n_pages * page
    assert blk % page == 0 and past % blk == 0 and n_pages <= LANES
    assert hd == LANES and page + 1 >= MAX_DISTANCE
    ppb = blk // page
    assert pps % ppb == 0
    nblk = past // blk
    rows = n_heads * tn
    cols = page * n_heads

    qh = q.reshape(b, tn, n_heads, hd).transpose(0, 2, 1, 3).reshape(b, rows, hd)
    kn = k_new.reshape(b, tn * n_heads, hd).astype(_BF16)
    vn = v_new.reshape(b, tn * n_heads, hd).astype(_BF16)

    head_of_row = np.repeat(np.arange(n_heads), tn)
    t_of_row = np.tile(np.arange(tn), n_heads)
    diag = jnp.asarray(head_of_row[:, None] == np.tile(np.arange(n_heads), page)[None, :])
    tab = _bias_table(rel_bias, page + tn)
    far = rel_bias[N_BUCKETS - 1].astype(_F32)
    bias_far = jnp.where(diag, far[head_of_row][:, None], MASKED)
    d_last = page + np.arange(tn)[:, None] - np.arange(page)[None, :]
    b_last = tab[:, d_last]
    b_last = jnp.broadcast_to(b_last[:, :, :, None], (n_heads, tn, page, n_heads)).reshape(rows, cols)
    bias_pages = jnp.stack([bias_far, jnp.where(diag, b_last, MASKED)]) * LOG2E
    d_new = t_of_row[:, None] - np.repeat(np.arange(tn), n_heads)[None, :]
    same_head = head_of_row[:, None] == np.tile(np.arange(n_heads), tn)[None, :]
    b_new = tab[head_of_row[:, None], np.maximum(d_new, 0)]
    bias_new = jnp.where(jnp.asarray(same_head & (d_new >= 0)), b_new, MASKED) * LOG2E
    qb = (qh * (hd ** -0.5 * LOG2E)).astype(_BF16)

    n_steps = n_pages // pps
    m_o, k_o = h_other.shape
    n_o = w_other.shape[1]
    tm = _tile(m_o, 512)
    tno = max(c for c in range(MM_CHUNK_COLS, 1024 + 1, MM_CHUNK_COLS)
              if n_o % c == 0 and pps % (c // MM_CHUNK_COLS) == 0)
    n_j, n_i = n_o // tno, m_o // tm
    assert n_j * n_i >= b * n_steps
    kernel = functools.partial(_attn_sample_kernel, n_heads=n_heads, tn=tn, pages_per_block=ppb, n_batch=b)
    unit = lambda j, i: _sample_unit(j, i, n_i, n_steps, b)
    page_spec = lambda s: pl.BlockSpec(
        (None, cols, hd), lambda j, i, pt: (pt[unit(j, i)[0], unit(j, i)[1] * pps + s], 0, 0))
    per_b = lambda shape: pl.BlockSpec((None,) + shape, lambda j, i, pt: (unit(j, i)[0], 0, 0))
    assert SUBLANES % tn == 0 and (b * tn) % SUBLANES == 0
    shared_rows = pl.BlockSpec((SUBLANES, width), lambda j, i, pt: (unit(j, i)[0] * tn // SUBLANES, 0))
    grid_spec = pltpu.PrefetchScalarGridSpec(
        num_scalar_prefetch=1,
        grid=(n_j, n_i),
        in_specs=[pl.BlockSpec((tm, k_o), lambda j, i, pt: (i, 0)),
                  pl.BlockSpec((k_o, tno), lambda j, i, pt: (0, j), pipeline_mode=pl.Buffered(1)),
                  per_b((rows, hd)), per_b((rows, hd))]
                 + [page_spec(s) for s in range(pps)] + [page_spec(s) for s in range(pps)]
                 + [pl.BlockSpec((None, rows, cols), lambda j, i, pt: (0, 0, 0)),
                    pl.BlockSpec((None, rows, cols), lambda j, i, pt: ((unit(j, i)[1] + 1) // n_steps, 0, 0)),
                    per_b((tn * n_heads, hd)), per_b((tn * n_heads, hd)),
                    pl.BlockSpec((rows, tn * n_heads), lambda j, i, pt: (0, 0)),
                    shared_rows],
        out_specs=[shared_rows, pl.BlockSpec((tm, tno), lambda j, i, pt: (i, j))],
        scratch_shapes=[pltpu.VMEM((n_pages, rows, hd), _F32), pltpu.VMEM((n_pages, rows, hd), _F32),
                        pltpu.VMEM((n_pages, rows, hd), _F32), pltpu.VMEM((nblk * n_heads, hd), _F32)],
    )
    attn, other = pl.pallas_call(
        kernel,
        grid_spec=grid_spec,
        out_shape=[jax.ShapeDtypeStruct((b * tn, width), _F32), jax.ShapeDtypeStruct((m_o, n_o), _BF16)],
        compiler_params=_params("arbitrary", "arbitrary"),
        name="attn_sample_and_proj",
    )(page_table, h_other, w_other, qh, qb, *([cache_k] * pps), *([cache_v] * pps),
      bias_pages, bias_pages, kn, vn, bias_new, rest)
    return attn.astype(_BF16), other


def _conv_combine(u, prev1, prev2, w_ref, b_ref, g_ref, o_ref):
    y = w_ref[0:1, :] * prev2 + w_ref[1:2, :] * prev1 + w_ref[2:3, :] * u
    g = g_ref[...].astype(_F32)
    o_ref[...] = (b_ref[...].astype(_F32) * y * (g * jax.nn.sigmoid(g))).astype(o_ref.dtype)


def _conv_prompt_kernel(b_ref, c_ref, h_ref, g_ref, ch_ref, hh_ref, w_ref, o_ref, st_ref):
    i = pl.program_id(1)
    u = c_ref[...].astype(_F32) * h_ref[...].astype(_F32)
    halo = ch_ref[...].astype(_F32) * hh_ref[...].astype(_F32) * (i > 0).astype(_F32)
    last, before = halo[HALO_ROWS - 1:HALO_ROWS, :], halo[HALO_ROWS - 2:HALO_ROWS - 1, :]
    row = lax.broadcasted_iota(jnp.int32, u.shape, 0)
    prev1 = jnp.where(row == 0, last, pltpu.roll(u, 1, axis=0))
    prev2 = jnp.where(row == 0, before, jnp.where(row == 1, last, pltpu.roll(u, 2, axis=0)))
    _conv_combine(u, prev1, prev2, w_ref, b_ref, g_ref, o_ref)
    st_ref[...] = u[u.shape[0] - SUBLANES:, :]


def _conv_sample_kernel(b_ref, c_ref, h_ref, g_ref, s1_ref, s2_ref, w_ref, o_ref, u_ref, *, tn):
    u = c_ref[...].astype(_F32) * h_ref[...].astype(_F32)
    t = lax.broadcasted_iota(jnp.int32, u.shape, 0) % tn
    prev1 = jnp.where(t >= 1, pltpu.roll(u, 1, axis=0), s1_ref[...])
    prev2 = jnp.where(t >= 2, pltpu.roll(u, 2, axis=0), s2_ref[...])
    _conv_combine(u, prev1, prev2, w_ref, b_ref, g_ref, o_ref)
    u_ref[...] = u


def _rest_cols(cw, tc):
    return [(1 + n) * cw // tc for n in range(4)]


def _conv_prompt(rest, conv_w, cw):
    t = rest.shape[0]
    tm = _tile(t, 512)
    tc = _tile(cw, 512)
    ob, oc, oh, og = _rest_cols(cw, tc)
    main = lambda off: pl.BlockSpec((tm, tc), lambda j, i: (i, off + j))
    halo = lambda off: pl.BlockSpec((HALO_ROWS, tc),
                                    lambda j, i: (jnp.maximum(i * (tm // HALO_ROWS) - 1, 0), off + j))
    return pl.pallas_call(
        _conv_prompt_kernel,
        grid=(cw // tc, t // tm),
        in_specs=[main(ob), main(oc), main(oh), main(og), halo(oc), halo(oh),
                  pl.BlockSpec((conv_w.shape[0], tc), lambda j, i: (0, j))],
        out_specs=[pl.BlockSpec((tm, tc), lambda j, i: (i, j)),
                   pl.BlockSpec((8, tc), lambda j, i: (0, j))],
        out_shape=[jax.ShapeDtypeStruct((t, cw), _BF16), jax.ShapeDtypeStruct((8, cw), _F32)],
        compiler_params=_params("parallel", "arbitrary"),
        name="conv_prompt",
    )(rest, rest, rest, rest, rest, rest, conv_w)


def _conv_sample(rest, state, conv_w, cw, tn):
    m = rest.shape[0]
    b = m // tn
    assert tn >= 2 and conv_w.shape[0] == 3
    tc = _tile(cw, 512)
    ob, oc, oh, og = _rest_cols(cw, tc)
    zeros = jnp.zeros((b, tn - 1, cw), _F32)
    s1 = jnp.concatenate([state[:, 1:2], zeros], axis=1).reshape(m, cw)
    s2 = jnp.concatenate([state, zeros[:, 1:]], axis=1).reshape(m, cw)
    main = lambda off: pl.BlockSpec((m, tc), lambda j: (0, off + j))
    own = pl.BlockSpec((m, tc), lambda j: (0, j))
    return pl.pallas_call(
        functools.partial(_conv_sample_kernel, tn=tn),
        grid=(cw // tc,),
        in_specs=[main(ob), main(oc), main(oh), main(og), own, own,
                  pl.BlockSpec((conv_w.shape[0], tc), lambda j: (0, j))],
        out_specs=[own, own],
        out_shape=[jax.ShapeDtypeStruct((m, cw), _BF16), jax.ShapeDtypeStruct((m, cw), _F32)],
        compiler_params=_params("parallel"),
        name="conv_sample",
    )(rest, rest, rest, rest, s1, s2, conv_w)


def _merge_kernel(a_ref, c_ref, wa_ref, wc_ref, ma_ref, mc_ref, o_ref):
    ya = jnp.dot(a_ref[...], wa_ref[...], preferred_element_type=_F32)
    yc = jnp.dot(c_ref[...], wc_ref[...], preferred_element_type=_F32)
    gate_a = jax.nn.sigmoid(ma_ref[...].astype(_F32))
    gate_c = jax.nn.sigmoid(mc_ref[...].astype(_F32))
    o_ref[...] = (gate_a * ya + gate_c * yc).astype(o_ref.dtype)


def _merge_cast_kernel(a_ref, c_ref, wa_ref, wc_ref, ma_ref, mc_ref, o_ref, wab_ref, wcb_ref):
    @pl.when(pl.program_id(1) == 0)
    def _():
        wab_ref[...] = wa_ref[...].astype(wab_ref.dtype)
        wcb_ref[...] = wc_ref[...].astype(wcb_ref.dtype)

    _merge_kernel(a_ref, c_ref, wab_ref, wcb_ref, ma_ref, mc_ref, o_ref)


def _merge(a, c, wa, wc, rest, gate_col0, cast=False):
    m, ka = a.shape
    kc = c.shape[1]
    d = wa.shape[1]
    tm = _tile(m, 512)
    tn = _tile(math.gcd(d, gate_col0), 512 if cast else 1024)
    oa = gate_col0 // tn
    oc = (gate_col0 + d) // tn
    out_spec = pl.BlockSpec((tm, tn), lambda j, i: (i, j))
    out_shape = jax.ShapeDtypeStruct((m, d), _BF16)
    if cast:
        out_spec = [out_spec, pl.BlockSpec((ka, tn), lambda j, i: (0, j)), pl.BlockSpec((kc, tn), lambda j, i: (0, j))]
        out_shape = [out_shape, jax.ShapeDtypeStruct((ka, d), _BF16), jax.ShapeDtypeStruct((kc, d), _BF16)]
    return pl.pallas_call(
        _merge_cast_kernel if cast else _merge_kernel,
        grid=(d // tn, m // tm),
        in_specs=[pl.BlockSpec((tm, ka), lambda j, i: (i, 0)),
                  pl.BlockSpec((tm, kc), lambda j, i: (i, 0)),
                  pl.BlockSpec((ka, tn), lambda j, i: (0, j)),
                  pl.BlockSpec((kc, tn), lambda j, i: (0, j)),
                  pl.BlockSpec((tm, tn), lambda j, i: (i, oa + j)),
                  pl.BlockSpec((tm, tn), lambda j, i: (i, oc + j))],
        out_specs=out_spec,
        out_shape=out_shape,
        compiler_params=_params("parallel", "arbitrary"),
        name="merge",
    )(a, c, wa, wc, rest, rest)


def _out_kernel(mg_ref, w_ref, x_ref, g_ref, o_ref, *maybe_wb_ref, tn, nj):
    j = pl.program_id(1)
    if maybe_wb_ref:
        maybe_wb_ref[0][...] = w_ref[...].astype(maybe_wb_ref[0].dtype)
        w_ref = maybe_wb_ref[0]
    z = x_ref[...] + jnp.dot(mg_ref[...], w_ref[...], preferred_element_type=_F32)
    for jj in range(nj):
        @pl.when(j == jj)
        def _(jj=jj):
            o_ref[:, jj * tn:(jj + 1) * tn] = z

    @pl.when(j == nj - 1)
    def _():
        nr = math.gcd(o_ref.shape[0], NORM_ROWS)

        def norm_rows(r, _):
            rows = pl.ds(pl.multiple_of(r * nr, nr), nr)
            y = o_ref[rows, :]
            ms = jnp.mean(y * y, axis=-1, keepdims=True)
            o_ref[rows, :] = (y * lax.rsqrt(ms + EPS)) * g_ref[...]
            return 0

        lax.fori_loop(0, o_ref.shape[0] // nr, norm_rows, 0)


def _out(merged, w, x, gain, cast=False):
    m, d = x.shape
    tm = _tile(m, 512)
    tn = _tile(d, 256 if cast else 1024)
    nj = d // tn
    out_spec = pl.BlockSpec((tm, d), lambda i, j: (i, 0))
    out_shape = jax.ShapeDtypeStruct((m, d), _F32)
    if cast:
        assert m == tm
        out_spec = [out_spec, pl.BlockSpec((d, tn), lambda i, j: (0, j))]
        out_shape = [out_shape, jax.ShapeDtypeStruct((d, d), _BF16)]
    return pl.pallas_call(
        functools.partial(_out_kernel, tn=tn, nj=nj),
        grid=(m // tm, nj),
        in_specs=[pl.BlockSpec((tm, d), lambda i, j: (i, 0)),
                  pl.BlockSpec((d, tn), lambda i, j: (0, j)),
                  pl.BlockSpec((tm, tn), lambda i, j: (i, j)),
                  pl.BlockSpec((1, d), lambda i, j: (0, 0))],
        out_specs=out_spec,
        out_shape=out_shape,
        compiler_params=_params("parallel", "arbitrary"),
        name="out_proj",
    )(merged, w, x, gain.reshape(1, d))


def _layer(x, h, weights, out_gain, aw, cw, attend, conv, rest=None):
    w_in, wa, wc, w_out = weights
    cast = not isinstance(w_in, (list, tuple))
    names = ("proj_q", "proj_k", "proj_v", "proj_rest")
    if cast:
        col0s = (0, aw, 2 * aw, 3 * aw)
        widths = (aw, aw, aw, w_in.shape[1] - 3 * aw)
        outs = [_proj(h, w_in, c0, n, name, cast=True) for c0, n, name in zip(col0s, widths, names)]
        (q, k, v, rest), w_in = zip(*outs)
    else:
        q, k, v = (_proj(h, w, 0, w.shape[1], name) for w, name in zip(w_in[:3], names))
        if rest is None:
            rest = _proj(h, w_in[3], 0, w_in[3].shape[1], names[3])
    a = attend(q, k, v, rest, w_in)
    c, state = conv(rest)
    if cast:
        merged, wa, wc = _merge(a, c, wa, wc, rest, aw + 4 * cw, cast=True)
        y, w_out = _out(merged, w_out, x, out_gain, cast=True)
    else:
        y = _out(_merge(a, c, wa, wc, rest, aw + 4 * cw), w_out, x, out_gain)
    return y, k, v, state, (list(w_in), wa, wc, w_out)


def kernel(x_prompt, x_sample, cache_k, cache_v, state_conv, page_table, norm_gain, w_in,
           conv_w, w_attn_out, w_conv_out, w_out, rel_bias, final_gain):
    depth = w_in.shape[0]
    assert depth == 1, "final norm is fused into the (single) layer's output projection"
    bp, t, d = x_prompt.shape
    assert bp == 1
    db, tn, _ = x_sample.shape
    n_pool, page, n_heads, hd = cache_k.shape[1:]
    aw = n_heads * hd
    cw = conv_w.shape[2]
    assert aw == cw

    l = 0
    ck = cache_k.reshape(depth * n_pool, page * n_heads, hd)
    cv = cache_v.reshape(depth * n_pool, page * n_heads, hd)

    def attend_prompt(q, k, v, rest, w_groups):
        return _attn_prompt(q, k, v, rest, rel_bias, n_heads, hd)

    def conv_prompt(rest):
        c, st = _conv_prompt(rest, conv_w[l], cw)
        return c, st[8 - (conv_w.shape[1] - 1):]

    h_prompt = _rmsnorm_bf16(x_prompt.reshape(t, d), norm_gain[l])
    rest_prompt = []

    def attend_sample(q, k, v, rest, w_groups):
        r3 = lambda z: z.reshape(db, tn, z.shape[-1])
        a, rest_p = _attn_sample(r3(q), r3(k), r3(v), rest, ck, cv, page_table + l * n_pool, rel_bias,
                                 n_heads, hd, h_prompt, w_groups[3])
        rest_prompt.append(rest_p)
        return a

    def conv_sample(rest):
        c, u = _conv_sample(rest, state_conv[l], conv_w[l], cw, tn)
        return c, u.reshape(db, tn, cw)[:, tn - (conv_w.shape[1] - 1):]

    weights = (w_in[l], w_attn_out[l], w_conv_out[l], w_out[l])
    xs = x_sample.reshape(db * tn, d)
    ys, ks, vs, cs, weights_b = _layer(xs, _rmsnorm_bf16(xs, norm_gain[l]), weights,
                                       final_gain, aw, cw, attend_sample, conv_sample)
    yp, kp, vp, cp, _ = _layer(x_prompt.reshape(t, d), h_prompt, weights_b,
                               final_gain, aw, cw, attend_prompt, conv_prompt, rest=rest_prompt[0])

    return (yp.reshape(1, t, d), ys.reshape(db, tn, d),
            kp.reshape(1, 1, t, n_heads, hd), vp.reshape(1, 1, t, n_heads, hd),
            cp.reshape(1, 1, conv_w.shape[1] - 1, cw),
            ks.reshape(1, db, tn, n_heads, hd), vs.reshape(1, db, tn, n_heads, hd),
            cs.reshape(1, db, conv_w.shape[1] - 1, cw))
```

```python
import functools
import math

import numpy as np
import jax
import jax.numpy as jnp
from jax import lax
from jax.experimental import pallas as pl
from jax.experimental.pallas import tpu as pltpu

MOBA_BLOCK = 256
MOBA_TOPK = 3
N_BUCKETS = 32
MAX_DISTANCE = 128
EPS = 1e-6
MASKED = -1e30
LANES = 128
SUBLANES = 8
HALO_ROWS = 16
SUM_ROWS = 16
NORM_ROWS = 64
LOG2E = math.log2(math.e)
FAR_GROUP = 2
HEADS_PER_STEP = 2
MM_CHUNK_COLS = 256
PAGES_PER_STEP = 8
VMEM_LIMIT_BYTES = 56 * 1024 * 1024

_BF16 = jnp.bfloat16
_F32 = jnp.float32


def _params(*semantics):
    return pltpu.CompilerParams(dimension_semantics=semantics, vmem_limit_bytes=VMEM_LIMIT_BYTES)


def _tile(dim, want):
    t = min(dim, want)
    while dim % t:
        t -= LANES
        assert t > 0, (dim, want)
    return t


def _rmsnorm_kernel(x_ref, g_ref, o_ref):
    x = x_ref[...]
    ms = jnp.mean(x * x, axis=-1, keepdims=True)
    o_ref[...] = ((x * lax.rsqrt(ms + EPS)) * g_ref[...]).astype(o_ref.dtype)


def _rmsnorm_bf16(x, gain):
    m, d = x.shape
    tm = _tile(m, 256)
    return pl.pallas_call(
        _rmsnorm_kernel,
        grid=(m // tm,),
        in_specs=[pl.BlockSpec((tm, d), lambda i: (i, 0)), pl.BlockSpec((1, d), lambda i: (0, 0))],
        out_specs=pl.BlockSpec((tm, d), lambda i: (i, 0)),
        out_shape=jax.ShapeDtypeStruct((m, d), _BF16),
        compiler_params=_params("parallel"),
        name="rmsnorm",
    )(x, gain.reshape(1, d))


def _proj_kernel(h_ref, w_ref, o_ref):
    o_ref[...] = jnp.dot(h_ref[...], w_ref[...], preferred_element_type=_F32).astype(o_ref.dtype)


def _proj_cast_kernel(h_ref, w_ref, o_ref, wb_ref):
    @pl.when(pl.program_id(1) == 0)
    def _():
        wb_ref[...] = w_ref[...].astype(wb_ref.dtype)

    o_ref[...] = jnp.dot(h_ref[...], wb_ref[...], preferred_element_type=_F32).astype(o_ref.dtype)


def _proj(h, w, col0, ncols, name, cast=False):
    m, k = h.shape
    tm = _tile(m, 1024)
    tn = _tile(math.gcd(ncols, col0), 512 if cast else 1024)
    c0 = col0 // tn
    in_specs = [pl.BlockSpec((tm, k), lambda j, i: (i, 0)),
                pl.BlockSpec((k, tn), lambda j, i: (0, c0 + j))]
    out_spec = pl.BlockSpec((tm, tn), lambda j, i: (i, j))
    out_shape = jax.ShapeDtypeStruct((m, ncols), _F32)
    if cast:
        out_spec = [out_spec, pl.BlockSpec((k, tn), lambda j, i: (0, j))]
        out_shape = [out_shape, jax.ShapeDtypeStruct((k, ncols), _BF16)]
    return pl.pallas_call(
        _proj_cast_kernel if cast else _proj_kernel,
        grid=(ncols // tn, m // tm),
        in_specs=in_specs,
        out_specs=out_spec,
        out_shape=out_shape,
        compiler_params=_params("parallel", "arbitrary"),
        name=name,
    )(h, w)


def _t5_bucket(dist):
    max_exact = N_BUCKETS // 2
    n = jnp.maximum(dist, 0)
    nf = jnp.maximum(n, 1).astype(_F32)
    large = max_exact + (jnp.log(nf / max_exact) / math.log(MAX_DISTANCE / max_exact)
                         * (N_BUCKETS - max_exact)).astype(jnp.int32)
    large = jnp.minimum(large, N_BUCKETS - 1)
    return jnp.where(n < max_exact, n, large)


def _bias_table(rel_bias, n_dist):
    return rel_bias[_t5_bucket(jnp.arange(n_dist, dtype=jnp.int32))].T.astype(_F32)


def _toeplitz(w):
    h, two_n = w.shape
    n = two_n // 2
    flat = jnp.tile(w, (1, n))[:, :n * (two_n - 1)]
    return flat.reshape(h, n, two_n - 1)[:, :, :n]


def _topk_mask(gate, n_valid, axis):
    idx = lax.broadcasted_iota(jnp.int32, gate.shape, axis).astype(_F32)
    g = jnp.where(idx < n_valid, gate, -jnp.inf)
    sel = jnp.zeros(gate.shape, _F32)
    for _ in range(MOBA_TOPK):
        mx = jnp.max(g, axis=axis, keepdims=True)
        first = jnp.min(jnp.where(g == mx, idx, float(gate.shape[axis])), axis=axis, keepdims=True)
        pick = idx == jnp.where(mx > -jnp.inf, first, -1.0)
        sel = jnp.where(pick, 1.0, sel)
        g = jnp.where(pick, -jnp.inf, g)
    return sel


def _topk_mask_by_rank(gate, n_valid):
    lane = lax.broadcasted_iota(jnp.int32, gate.shape, 1)
    rank = jnp.zeros(gate.shape, _F32)
    for j in range(n_valid):
        gj = jnp.broadcast_to(gate[:, j:j + 1], gate.shape)
        rank = rank + jnp.where(gj > gate, 1.0, jnp.where(gj == gate, jnp.where(lane > j, 1.0, 0.0), 0.0))
    return jnp.where(lane < n_valid, jnp.where(rank < float(MOBA_TOPK), 1.0, 0.0), 0.0)


def _split_bf16(x, keep_remainder=False):
    hi = x.astype(_BF16)
    lo = x - hi.astype(_F32)
    return hi, (lo if keep_remainder else lo.astype(_BF16))


def _nt_dot(a, b, precision=None):
    return lax.dot_general(a, b, (((1,), (1,)), ((), ())), precision=precision, preferred_element_type=_F32)


def _attn_prompt_kernel(far_ref, q_ref, k_ref, v_ref, g_ref, bown_ref, bprev_ref, o_ref,
                        kb_ref, vt_ref, km_ref, sel_ref, s_ref, *, qscale):
    hp = pl.program_id(0)
    i = pl.program_id(1)
    n_h, nb, blk, hd = kb_ref.shape
    head_cols = [slice(e * hd, (e + 1) * hd) for e in range(n_h)]

    @pl.when(i == 0)
    def _():
        def load_block(c, _):
            rows = pl.ds(pl.multiple_of(c * blk, blk), blk)
            for e in range(n_h):
                kf = k_ref[rows, head_cols[e]]
                kb_ref[e, c] = kf.astype(_BF16)
                vt_ref[e, c, :hd, :] = v_ref[rows, head_cols[e]].T.astype(_BF16)
                extra = lax.broadcasted_iota(jnp.int32, (SUM_ROWS, blk), 0) == 0
                vt_ref[e, c, hd:, :] = jnp.where(extra, 1.0, 0.0).astype(_BF16)
                km_ref[e, pl.ds(c, 1), :] = jnp.mean(kf, axis=0, keepdims=True)
            return 0

        lax.fori_loop(0, nb, load_block, 0)

    def scores(e, j, qtb):
        return jnp.dot(kb_ref[e, j], qtb, preferred_element_type=_F32)

    def weighted_values(e, j, p):
        return jnp.dot(vt_ref[e, j], p.astype(_BF16), preferred_element_type=_F32)

    def far_block(gidx, g):
        return jnp.minimum(gidx * FAR_GROUP + g, nb - 1)

    def far_scores(buf, gidx):
        for e in range(n_h):
            for g in range(FAR_GROUP):
                s_ref[buf, e, g] = scores(e, far_block(gidx, g), qtbs[e])

    def far_bias_row(e, gidx, g):
        visible = gidx * FAR_GROUP + g < i - 1
        return jnp.where(sel_ref[e, pl.ds(far_block(gidx, g), 1), :] > 0.0,
                         jnp.where(visible, far_biases[e], MASKED), MASKED)

    def far_softmax(buf, gidx, carry):
        out = []
        for e, (m, acc) in enumerate(carry):
            pens = [far_bias_row(e, gidx, g) for g in range(FAR_GROUP)]
            m_new = m
            for g in range(FAR_GROUP):
                m_new = jnp.maximum(m_new, jnp.max(s_ref[buf, e, g], axis=0, keepdims=True) + pens[g])
            acc = jnp.exp2(m - m_new) * acc
            for g in range(FAR_GROUP):
                p = jnp.exp2(s_ref[buf, e, g] - (m_new - pens[g]))
                acc = acc + weighted_values(e, far_block(gidx, g), p)
            out.append((m_new, acc))
        return tuple(out)

    jp =jnp.maximum(i - 1, 0)
    far_biases = [far_ref[hp * n_h + e] for e in range(n_h)]
    qts = [q_ref[:, head_cols[e]].T for e in range(n_h)]
    qtbs = [(qt * qscale).astype(_BF16) for qt in qts]
    s_own = [scores(e, i, qtbs[e]) + bown_ref[e] for e in range(n_h)]
    gates = []
    for e in range(n_h):
        q_hi, q_lo = _split_bf16(qts[e])
        km_hi, km_lo = _split_bf16(km_ref[e])
        gates.append(jnp.dot(km_hi, q_hi, preferred_element_type=_F32)
                     + jnp.dot(km_lo, q_hi, preferred_element_type=_F32)
                     + jnp.dot(km_hi, q_lo, preferred_element_type=_F32))
    far_scores(0, 0)
    s_prev = [scores(e, jp, qtbs[e]) + bprev_ref[e] for e in range(n_h)]
    carry = []
    for e in range(n_h):
        m = jnp.max(s_own[e], axis=0, keepdims=True)
        acc = weighted_values(e, i, jnp.exp2(s_own[e] - m))
        sel_ref[e] = _topk_mask(gates[e], i.astype(_F32), axis=0)
        pen_prev = jnp.where(sel_ref[e, pl.ds(jp, 1), :] > 0.0, 0.0, MASKED)
        m_new = jnp.maximum(m, jnp.max(s_prev[e], axis=0, keepdims=True) + pen_prev)
        acc = jnp.exp2(m - m_new) * acc + weighted_values(e, jp, jnp.exp2(s_prev[e] - (m_new - pen_prev)))
        carry.append((m_new, acc))

    def far_pair(t, carry):
        far_scores(1, 2 * t + 1)
        carry = far_softmax(0, 2 * t, carry)
        far_scores(0, 2 * t + 2)
        return far_softmax(1, 2 * t + 1, carry)

    n_groups = (jnp.maximum(i - 1, 0) + FAR_GROUP - 1) // FAR_GROUP
    far_scores(0, 0)
    carry = lax.fori_loop(0, (n_groups + 1) // 2, far_pair, tuple(carry))
    for e, (m, acc) in enumerate(carry):
        g = g_ref[:, head_cols[e]].astype(_F32)
        out = acc[:hd, :] / acc[hd:hd + 1, :]
        o_ref[:, head_cols[e]] = (out.T * (g * jax.nn.sigmoid(g))).astype(o_ref.dtype)


def _attn_prompt(q, k, v, rest, rel_bias, n_heads, hd):
    t = q.shape[0]
    blk = MOBA_BLOCK
    assert t % blk == 0 and hd == LANES
    nb = t // blk
    assert blk + 1 >= MAX_DISTANCE
    tab = _bias_table(rel_bias, 2 * blk)
    bown = _toeplitz(jnp.concatenate([tab[:, :blk], jnp.full((n_heads, blk), MASKED, _F32)], axis=1)) * LOG2E
    bprev = _toeplitz(jnp.concatenate([tab[:, blk:], tab[:, :blk]], axis=1)) * LOG2E
    far = rel_bias[N_BUCKETS - 1].astype(_F32) * LOG2E
    n_h = math.gcd(n_heads, HEADS_PER_STEP)
    kernel = functools.partial(_attn_prompt_kernel, qscale=hd ** -0.5 * LOG2E)
    return pl.pallas_call(
        kernel,
        grid=(n_heads // n_h, nb),
        in_specs=[pl.BlockSpec(memory_space=pltpu.SMEM),
                  pl.BlockSpec((blk, n_h * hd), lambda h, i: (i, h)),
                  pl.BlockSpec((t, n_h * hd), lambda h, i: (0, h)),
                  pl.BlockSpec((t, n_h * hd), lambda h, i: (0, h)),
                  pl.BlockSpec((blk, n_h * hd), lambda h, i: (i, h)),
                  pl.BlockSpec((n_h, blk, blk), lambda h, i: (h, 0, 0)),
                  pl.BlockSpec((n_h, blk, blk), lambda h, i: (h, 0, 0))],
        out_specs=pl.BlockSpec((blk, n_h * hd), lambda h, i: (i, h)),
        out_shape=jax.ShapeDtypeStruct((t, n_heads * hd), _BF16),
        scratch_shapes=[pltpu.VMEM((n_h, nb, blk, hd), _BF16), pltpu.VMEM((n_h, nb, hd + SUM_ROWS, blk), _BF16),
                        pltpu.VMEM((n_h, nb, hd), _F32), pltpu.VMEM((n_h, nb, blk), _F32),
                        pltpu.VMEM((2, n_h, FAR_GROUP, blk, blk), _F32)],
        compiler_params=_params("parallel", "arbitrary"),
        name="attn_prompt",
    )(far, q, k, v, rest, bown, bprev)


def _sample_unit(j, i, n_i, n_steps, n_batch):
    step = j * n_i + i
    b = step // n_steps
    return jnp.minimum(b, n_batch - 1), jnp.where(b < n_batch, step % n_steps, n_steps - 1)


def _attn_sample_kernel(pt_ref, h_ref, w_ref, q_ref, qb_ref, *refs, n_heads, tn, pages_per_block, n_batch):
    pps = (len(refs) - 12) // 2
    k_refs, v_refs = refs[:pps], refs[pps:2 * pps]
    (bfar_ref, blast_ref, kn_ref, vn_ref, bnew_ref, g_ref, o_ref, orest_ref,
     m_ref, l_ref, pv_ref, ks_ref) = refs[2 * pps:]
    del pt_ref
    n_pages, rows, hd = pv_ref.shape
    nblk = ks_ref.shape[0] // n_heads
    n_steps = n_pages // pps
    bi, jj = _sample_unit(pl.program_id(0), pl.program_id(1), pl.num_programs(1), n_steps, n_batch)
    qb = qb_ref[...]
    n_chunks = w_ref.shape[1] // MM_CHUNK_COLS
    assert w_ref.shape[1] % MM_CHUNK_COLS == 0 and pps % n_chunks == 0
    group = pps // n_chunks
    for c in range(n_chunks):
        members = range(c * group, (c + 1) * group)
        kfs = {s: k_refs[s][...] for s in members}
        scs = {s: _nt_dot(qb, kfs[s].astype(_BF16)) + (blast_ref if s == pps - 1 else bfar_ref)[...]
               for s in members}
        chunk = slice(c * MM_CHUNK_COLS, (c + 1) * MM_CHUNK_COLS)
        orest_ref[:, chunk] = jnp.dot(h_ref[...], w_ref[:, chunk],
                                      preferred_element_type=_F32).astype(orest_ref.dtype)
        for s in members:
            page = jj * pps + s
            mp = jnp.max(scs[s], axis=-1, keepdims=True)
            p = jnp.exp2(scs[s] - mp)
            lp = jnp.sum(p, axis=-1, keepdims=True)
            pv_ref[page] = jnp.dot(p.astype(_BF16), v_refs[s][...].astype(_BF16), preferred_element_type=_F32)
            m_ref[page] = jnp.broadcast_to(mp, (rows, hd))
            l_ref[page] = jnp.broadcast_to(lp, (rows, hd))
            ksum = jnp.sum(kfs[s].reshape(kfs[s].shape[0] // n_heads, n_heads, hd), axis=0)
            blk_rows = pl.ds(pl.multiple_of((page // pages_per_block) * n_heads, n_heads), n_heads)
            if s % pages_per_block == 0:
                ks_ref[blk_rows, :] = ksum
            else:
                ks_ref[blk_rows, :] = ks_ref[blk_rows, :] + ksum

    @pl.when(jj == n_steps - 1)
    def _():
        q_hi, q_lo = _split_bf16(q_ref[...])
        ks_hi, ks_lo = _split_bf16(ks_ref[...])
        g2 = _nt_dot(q_hi, ks_hi) + _nt_dot(q_lo, ks_hi) + _nt_dot(q_hi, ks_lo)
        row_head = lax.broadcasted_iota(jnp.int32, g2.shape, 0) // tn
        col_head = lax.broadcasted_iota(jnp.int32, g2.shape, 1) % n_heads
        own_head = jnp.where(col_head == row_head, g2, 0.0)
        fold = (lax.broadcasted_iota(jnp.int32, (g2.shape[1], LANES), 0) // n_heads
                == lax.broadcasted_iota(jnp.int32, (g2.shape[1], LANES), 1)).astype(_BF16)
        t_hi, rest_lo = _split_bf16(own_head, keep_remainder=True)
        t_mid, t_lo = _split_bf16(rest_lo)
        gate = (jnp.dot(t_hi, fold, preferred_element_type=_F32) + jnp.dot(t_mid, fold, preferred_element_type=_F32)
                + jnp.dot(t_lo, fold, preferred_element_type=_F32))
        sel = _topk_mask_by_rank(gate, nblk)

        s_new = _nt_dot(qb, kn_ref[...]) + bnew_ref[...]
        m_new = jnp.max(s_new, axis=-1, keepdims=True)
        live = [jnp.broadcast_to(sel[:, b:b + 1], (rows, hd)) > 0.0 for b in range(nblk)]
        m_tot = jnp.broadcast_to(m_new, (rows, hd))
        for pg in range(n_pages):
            m_tot = jnp.maximum(m_tot, jnp.where(live[pg // pages_per_block], m_ref[pg], MASKED))
        p_new = jnp.exp2(s_new - m_tot[:, :s_new.shape[1]])
        l = jnp.broadcast_to(jnp.sum(p_new, axis=-1, keepdims=True), (rows, hd))
        acc = jnp.dot(p_new.astype(_BF16), vn_ref[...], preferred_element_type=_F32)
        for pg in range(n_pages):
            w = jnp.where(live[pg // pages_per_block], jnp.exp2(m_ref[pg] - m_tot), 0.0)
            l = l + w * l_ref[pg]
            acc = acc + w * pv_ref[pg]
        out = acc / l
        for pos in range(SUBLANES // tn):
            @pl.when(bi % (SUBLANES // tn) == pos)
            def _(pos=pos):
                mine = slice(pos * tn, (pos + 1) * tn)
                g = g_ref[mine, :].astype(_F32)
                gated = g * jax.nn.sigmoid(g)
                for hh in range(n_heads):
                    cols = slice(hh * hd, (hh + 1) * hd)
                    o_ref[mine, cols] = (out[hh * tn:(hh + 1) * tn, :] * gated[:, cols]).astype(o_ref.dtype)


def _attn_sample(q, k_new, v_new, rest, cache_k, cache_v, page_table, rel_bias, n_heads, hd, h_other, w_other):
    b, tn, width = q.shape
    n_pages = page_table.shape[1]
    page = cache_k.shape[1] // n_heads
    blk = MOBA_BLOCK
    pps = math.gcd(n_pages, PAGES_PER_STEP)
    past = n_pages * page
    assert blk % page == 0 and past % blk == 0 and n_pages <= LANES
    assert hd == LANES and page + 1 >= MAX_DISTANCE
    ppb = blk // page
    assert pps % ppb == 0
    nblk = past // blk
    rows = n_heads * tn
    cols = page * n_heads

    qh = q.reshape(b, tn, n_heads, hd).transpose(0, 2, 1, 3).reshape(b, rows, hd)
    kn = k_new.reshape(b, tn * n_heads, hd).astype(_BF16)
    vn = v_new.reshape(b, tn * n_heads, hd).astype(_BF16)

    head_of_row = np.repeat(np.arange(n_heads), tn)
    t_of_row = np.tile(np.arange(tn), n_heads)
    diag = jnp.asarray(head_of_row[:, None] == np.tile(np.arange(n_heads), page)[None, :])
    tab = _bias_table(rel_bias, page + tn)
    far = rel_bias[N_BUCKETS - 1].astype(_F32)
    bias_far = jnp.where(diag, far[head_of_row][:, None], MASKED)
    d_last = page + np.arange(tn)[:, None] - np.arange(page)[None, :]
    b_last = tab[:, d_last]
    b_last = jnp.broadcast_to(b_last[:, :, :, None], (n_heads, tn, page, n_heads)).reshape(rows, cols)
    bias_pages = jnp.stack([bias_far, jnp.where(diag, b_last, MASKED)]) * LOG2E
    d_new = t_of_row[:, None] - np.repeat(np.arange(tn), n_heads)[None, :]
    same_head = head_of_row[:, None] == np.tile(np.arange(n_heads), tn)[None, :]
    b_new = tab[head_of_row[:, None], np.maximum(d_new, 0)]
    bias_new = jnp.where(jnp.asarray(same_head & (d_new >= 0)), b_new, MASKED) * LOG2E
    qb = (qh * (hd ** -0.5 * LOG2E)).astype(_BF16)

    n_steps = n_pages // pps
    m_o, k_o = h_other.shape
    n_o = w_other.shape[1]
    tm = _tile(m_o, 512)
    tno = max(c for c in range(MM_CHUNK_COLS, 1024 + 1, MM_CHUNK_COLS)
              if n_o % c == 0 and pps % (c // MM_CHUNK_COLS) == 0)
    n_j, n_i = n_o // tno, m_o // tm
    assert n_j * n_i >= b * n_steps
    kernel = functools.partial(_attn_sample_kernel, n_heads=n_heads, tn=tn, pages_per_block=ppb, n_batch=b)
    unit = lambda j, i: _sample_unit(j, i, n_i, n_steps, b)
    page_spec = lambda s: pl.BlockSpec(
        (None, cols, hd), lambda j, i, pt: (pt[unit(j, i)[0], unit(j, i)[1] * pps + s], 0, 0))
    per_b = lambda shape: pl.BlockSpec((None,) + shape, lambda j, i, pt: (unit(j, i)[0], 0, 0))
    assert SUBLANES % tn == 0 and (b * tn) % SUBLANES == 0
    shared_rows = pl.BlockSpec((SUBLANES, width), lambda j, i, pt: (unit(j, i)[0] * tn // SUBLANES, 0))
    grid_spec = pltpu.PrefetchScalarGridSpec(
        num_scalar_prefetch=1,
        grid=(n_j, n_i),
        in_specs=[pl.BlockSpec((tm, k_o), lambda j, i, pt: (i, 0)),
                  pl.BlockSpec((k_o, tno), lambda j, i, pt: (0, j), pipeline_mode=pl.Buffered(1)),
                  per_b((rows, hd)), per_b((rows, hd))]
                 + [page_spec(s) for s in range(pps)] + [page_spec(s) for s in range(pps)]
                 + [pl.BlockSpec((None, rows, cols), lambda j, i, pt: (0, 0, 0)),
                    pl.BlockSpec((None, rows, cols), lambda j, i, pt: ((unit(j, i)[1] + 1) // n_steps, 0, 0)),
                    per_b((tn * n_heads, hd)), per_b((tn * n_heads, hd)),
                    pl.BlockSpec((rows, tn * n_heads), lambda j, i, pt: (0, 0)),
                    shared_rows],
        out_specs=[shared_rows, pl.BlockSpec((tm, tno), lambda j, i, pt: (i, j))],
        scratch_shapes=[pltpu.VMEM((n_pages, rows, hd), _F32), pltpu.VMEM((n_pages, rows, hd), _F32),
                        pltpu.VMEM((n_pages, rows, hd), _F32), pltpu.VMEM((nblk * n_heads, hd), _F32)],
    )
    attn, other = pl.pallas_call(
        kernel,
        grid_spec=grid_spec,
        out_shape=[jax.ShapeDtypeStruct((b * tn, width), _F32), jax.ShapeDtypeStruct((m_o, n_o), _BF16)],
        compiler_params=_params("arbitrary", "arbitrary"),
        name="attn_sample_and_proj",
    )(page_table, h_other, w_other, qh, qb, *([cache_k] * pps), *([cache_v] * pps),
      bias_pages, bias_pages, kn, vn, bias_new, rest)
    return attn.astype(_BF16), other


def _conv_combine(u, prev1, prev2, w_ref, b_ref, g_ref, o_ref):
    y = w_ref[0:1, :] * prev2 + w_ref[1:2, :] * prev1 + w_ref[2:3, :] * u
    g = g_ref[...].astype(_F32)
    o_ref[...] = (b_ref[...].astype(_F32) * y * (g * jax.nn.sigmoid(g))).astype(o_ref.dtype)


def _conv_prompt_kernel(b_ref, c_ref, h_ref, g_ref, ch_ref, hh_ref, w_ref, o_ref, st_ref):
    i = pl.program_id(1)
    u = c_ref[...].astype(_F32) * h_ref[...].astype(_F32)
    halo = ch_ref[...].astype(_F32) * hh_ref[...].astype(_F32) * (i > 0).astype(_F32)
    last, before = halo[HALO_ROWS - 1:HALO_ROWS, :], halo[HALO_ROWS - 2:HALO_ROWS - 1, :]
    row = lax.broadcasted_iota(jnp.int32, u.shape, 0)
    prev1 = jnp.where(row == 0, last, pltpu.roll(u, 1, axis=0))
    prev2 = jnp.where(row == 0, before, jnp.where(row == 1, last, pltpu.roll(u, 2, axis=0)))
    _conv_combine(u, prev1, prev2, w_ref, b_ref, g_ref, o_ref)
    st_ref[...] = u[u.shape[0] - SUBLANES:, :]


def _conv_sample_kernel(b_ref, c_ref, h_ref, g_ref, s1_ref, s2_ref, w_ref, o_ref, u_ref, *, tn):
    u = c_ref[...].astype(_F32) * h_ref[...].astype(_F32)
    t = lax.broadcasted_iota(jnp.int32, u.shape, 0) % tn
    prev1 = jnp.where(t >= 1, pltpu.roll(u, 1, axis=0), s1_ref[...])
    prev2 = jnp.where(t >= 2, pltpu.roll(u, 2, axis=0), s2_ref[...])
    _conv_combine(u, prev1, prev2, w_ref, b_ref, g_ref, o_ref)
    u_ref[...] = u


def _rest_cols(cw, tc):
    return [(1 + n) * cw // tc for n in range(4)]


def _conv_prompt(rest, conv_w, cw):
    t = rest.shape[0]
    tm = _tile(t, 512)
    tc = _tile(cw, 512)
    ob, oc, oh, og = _rest_cols(cw, tc)
    main = lambda off: pl.BlockSpec((tm, tc), lambda j, i: (i, off + j))
    halo = lambda off: pl.BlockSpec((HALO_ROWS, tc),
                                    lambda j, i: (jnp.maximum(i * (tm // HALO_ROWS) - 1, 0), off + j))
    return pl.pallas_call(
        _conv_prompt_kernel,
        grid=(cw // tc, t // tm),
        in_specs=[main(ob), main(oc), main(oh), main(og), halo(oc), halo(oh),
                  pl.BlockSpec((conv_w.shape[0], tc), lambda j, i: (0, j))],
        out_specs=[pl.BlockSpec((tm, tc), lambda j, i: (i, j)),
                   pl.BlockSpec((8, tc), lambda j, i: (0, j))],
        out_shape=[jax.ShapeDtypeStruct((t, cw), _BF16), jax.ShapeDtypeStruct((8, cw), _F32)],
        compiler_params=_params("parallel", "arbitrary"),
        name="conv_prompt",
    )(rest, rest, rest, rest, rest, rest, conv_w)


def _conv_sample(rest, state, conv_w, cw, tn):
    m = rest.shape[0]
    b = m // tn
    assert tn >= 2 and conv_w.shape[0] == 3
    tc = _tile(cw, 512)
    ob, oc, oh, og = _rest_cols(cw, tc)
    zeros = jnp.zeros((b, tn - 1, cw), _F32)
    s1 = jnp.concatenate([state[:, 1:2], zeros], axis=1).reshape(m, cw)
    s2 = jnp.concatenate([state, zeros[:, 1:]], axis=1).reshape(m, cw)
    main = lambda off: pl.BlockSpec((m, tc), lambda j: (0, off + j))
    own = pl.BlockSpec((m, tc), lambda j: (0, j))
    return pl.pallas_call(
        functools.partial(_conv_sample_kernel, tn=tn),
        grid=(cw // tc,),
        in_specs=[main(ob), main(oc), main(oh), main(og), own, own,
                  pl.BlockSpec((conv_w.shape[0], tc), lambda j: (0, j))],
        out_specs=[own, own],
        out_shape=[jax.ShapeDtypeStruct((m, cw), _BF16), jax.ShapeDtypeStruct((m, cw), _F32)],
        compiler_params=_params("parallel"),
        name="conv_sample",
    )(rest, rest, rest, rest, s1, s2, conv_w)


def _merge_kernel(a_ref, c_ref, wa_ref, wc_ref, ma_ref, mc_ref, o_ref):
    ya = jnp.dot(a_ref[...], wa_ref[...], preferred_element_type=_F32)
    yc = jnp.dot(c_ref[...], wc_ref[...], preferred_element_type=_F32)
    gate_a = jax.nn.sigmoid(ma_ref[...].astype(_F32))
    gate_c = jax.nn.sigmoid(mc_ref[...].astype(_F32))
    o_ref[...] = (gate_a * ya + gate_c * yc).astype(o_ref.dtype)


def _merge_cast_kernel(a_ref, c_ref, wa_ref, wc_ref, ma_ref, mc_ref, o_ref, wab_ref, wcb_ref):
    @pl.when(pl.program_id(1) == 0)
    def _():
        wab_ref[...] = wa_ref[...].astype(wab_ref.dtype)
        wcb_ref[...] = wc_ref[...].astype(wcb_ref.dtype)

    _merge_kernel(a_ref, c_ref, wab_ref, wcb_ref, ma_ref, mc_ref, o_ref)


def _merge(a, c, wa, wc, rest, gate_col0, cast=False):
    m, ka = a.shape
    kc = c.shape[1]
    d = wa.shape[1]
    tm = _tile(m, 512)
    tn = _tile(math.gcd(d, gate_col0), 512 if cast else 1024)
    oa = gate_col0 // tn
    oc = (gate_col0 + d) // tn
    out_spec = pl.BlockSpec((tm, tn), lambda j, i: (i, j))
    out_shape = jax.ShapeDtypeStruct((m, d), _BF16)
    if cast:
        out_spec = [out_spec, pl.BlockSpec((ka, tn), lambda j, i: (0, j)), pl.BlockSpec((kc, tn), lambda j, i: (0, j))]
        out_shape = [out_shape, jax.ShapeDtypeStruct((ka, d), _BF16), jax.ShapeDtypeStruct((kc, d), _BF16)]
    return pl.pallas_call(
        _merge_cast_kernel if cast else _merge_kernel,
        grid=(d // tn, m // tm),
        in_specs=[pl.BlockSpec((tm, ka), lambda j, i: (i, 0)),
                  pl.BlockSpec((tm, kc), lambda j, i: (i, 0)),
                  pl.BlockSpec((ka, tn), lambda j, i: (0, j)),
                  pl.BlockSpec((kc, tn), lambda j, i: (0, j)),
                  pl.BlockSpec((tm, tn), lambda j, i: (i, oa + j)),
                  pl.BlockSpec((tm, tn), lambda j, i: (i, oc + j))],
        out_specs=out_spec,
        out_shape=out_shape,
        compiler_params=_params("parallel", "arbitrary"),
        name="merge",
    )(a, c, wa, wc, rest, rest)


def _out_kernel(mg_ref, w_ref, x_ref, g_ref, o_ref, *maybe_wb_ref, tn, nj):
    j = pl.program_id(1)
    if maybe_wb_ref:
        maybe_wb_ref[0][...] = w_ref[...].astype(maybe_wb_ref[0].dtype)
        w_ref = maybe_wb_ref[0]
    z = x_ref[...] + jnp.dot(mg_ref[...], w_ref[...], preferred_element_type=_F32)
    for jj in range(nj):
        @pl.when(j == jj)
        def _(jj=jj):
            o_ref[:, jj * tn:(jj + 1) * tn] = z

    @pl.when(j == nj - 1)
    def _():
        nr = math.gcd(o_ref.shape[0], NORM_ROWS)

        def norm_rows(r, _):
            rows = pl.ds(pl.multiple_of(r * nr, nr), nr)
            y = o_ref[rows, :]
            ms = jnp.mean(y * y, axis=-1, keepdims=True)
            o_ref[rows, :] = (y * lax.rsqrt(ms + EPS)) * g_ref[...]
            return 0

        lax.fori_loop(0, o_ref.shape[0] // nr, norm_rows, 0)


def _out(merged, w, x, gain, cast=False):
    m, d = x.shape
    tm = _tile(m, 512)
    tn = _tile(d, 256 if cast else 1024)
    nj = d // tn
    out_spec = pl.BlockSpec((tm, d), lambda i, j: (i, 0))
    out_shape = jax.ShapeDtypeStruct((m, d), _F32)
    if cast:
        assert m == tm
        out_spec = [out_spec, pl.BlockSpec((d, tn), lambda i, j: (0, j))]
        out_shape = [out_shape, jax.ShapeDtypeStruct((d, d), _BF16)]
    return pl.pallas_call(
        functools.partial(_out_kernel, tn=tn, nj=nj),
        grid=(m // tm, nj),
        in_specs=[pl.BlockSpec((tm, d), lambda i, j: (i, 0)),
                  pl.BlockSpec((d, tn), lambda i, j: (0, j)),
                  pl.BlockSpec((tm, tn), lambda i, j: (i, j)),
                  pl.BlockSpec((1, d), lambda i, j: (0, 0))],
        out_specs=out_spec,
        out_shape=out_shape,
        compiler_params=_params("parallel", "arbitrary"),
        name="out_proj",
    )(merged, w, x, gain.reshape(1, d))


def _layer(x, h, weights, out_gain, aw, cw, attend, conv, rest=None):
    w_in, wa, wc, w_out = weights
    cast = not isinstance(w_in, (list, tuple))
    names = ("proj_q", "proj_k", "proj_v", "proj_rest")
    if cast:
        col0s = (0, aw, 2 * aw, 3 * aw)
        widths = (aw, aw, aw, w_in.shape[1] - 3 * aw)
        outs = [_proj(h, w_in, c0, n, name, cast=True) for c0, n, name in zip(col0s, widths, names)]
        (q, k, v, rest), w_in = zip(*outs)
    else:
        q, k, v = (_proj(h, w, 0, w.shape[1], name) for w, name in zip(w_in[:3], names))
        if rest is None:
            rest = _proj(h, w_in[3], 0, w_in[3].shape[1], names[3])
    a = attend(q, k, v, rest, w_in)
    c, state = conv(rest)
    if cast:
        merged, wa, wc = _merge(a, c, wa, wc, rest, aw + 4 * cw, cast=True)
        y, w_out = _out(merged, w_out, x, out_gain, cast=True)
    else:
        y = _out(_merge(a, c, wa, wc, rest, aw + 4 * cw), w_out, x, out_gain)
    return y, k, v, state, (list(w_in), wa, wc, w_out)


def kernel(x_prompt, x_sample, cache_k, cache_v, state_conv, page_table, norm_gain, w_in,
           conv_w, w_attn_out, w_conv_out, w_out, rel_bias, final_gain):
    depth = w_in.shape[0]
    assert depth == 1, "final norm is fused into the (single) layer's output projection"
    bp, t, d = x_prompt.shape
    assert bp == 1
    db, tn, _ = x_sample.shape
    n_pool, page, n_heads, hd = cache_k.shape[1:]
    aw = n_heads * hd
    cw = conv_w.shape[2]
    assert aw == cw

    l = 0
    ck = cache_k.reshape(depth * n_pool, page * n_heads, hd)
    cv = cache_v.reshape(depth * n_pool, page * n_heads, hd)

    def attend_prompt(q, k, v, rest, w_groups):
        return _attn_prompt(q, k, v, rest, rel_bias, n_heads, hd)

    def conv_prompt(rest):
        c, st = _conv_prompt(rest, conv_w[l], cw)
        return c, st[8 - (conv_w.shape[1] - 1):]

    h_prompt = _rmsnorm_bf16(x_prompt.reshape(t, d), norm_gain[l])
    rest_prompt = []

    def attend_sample(q, k, v, rest, w_groups):
        r3 = lambda z: z.reshape(db, tn, z.shape[-1])
        a, rest_p = _attn_sample(r3(q), r3(k), r3(v), rest, ck, cv, page_table + l * n_pool, rel_bias,
                                 n_heads, hd, h_prompt, w_groups[3])
        rest_prompt.append(rest_p)
        return a

    def conv_sample(rest):
        c, u = _conv_sample(rest, state_conv[l], conv_w[l], cw, tn)
        return c, u.reshape(db, tn, cw)[:, tn - (conv_w.shape[1] - 1):]

    weights = (w_in[l], w_attn_out[l], w_conv_out[l], w_out[l])
    xs = x_sample.reshape(db * tn, d)
    ys, ks, vs, cs, weights_b = _layer(xs, _rmsnorm_bf16(xs, norm_gain[l]), weights,
                                       final_gain, aw, cw, attend_sample, conv_sample)
    yp, kp, vp, cp, _ = _layer(x_prompt.reshape(t, d), h_prompt, weights_b,
                               final_gain, aw, cw, attend_prompt, conv_prompt, rest=rest_prompt[0])

    return (yp.reshape(1, t, d), ys.reshape(db, tn, d),
            kp.reshape(1, 1, t, n_heads, hd), vp.reshape(1, 1, t, n_heads, hd),
            cp.reshape(1, 1, conv_w.shape[1] - 1, cw),
            ks.reshape(1, db, tn, n_heads, hd), vs.reshape(1, db, tn, n_heads, hd),
            cs.reshape(1, db, conv_w.shape[1] - 1, cw))
```
